```python
import math
import jax
import jax.numpy as jnp
from jax import lax
import numpy as np

D_MODEL = 1024
BATCH = 1
SEQ = 16384
DEPTH = 2

GRID_W = 64
CTX_LEN = 256
N_EVEN = (DEPTH + 1) // 2
N_ODD = DEPTH // 2
D_S5 = D_MODEL // 2
S5_GROUP = 16
S5_GROUPS = D_S5 // S5_GROUP
S5_STATE = 64
DT_MIN = 0.001
DT_MAX = 0.1
D_POOL = D_MODEL - D_S5
POOL_WINDOWS = (2, 4, 8, 16)
POOL_GROUP = D_POOL // len(POOL_WINDOWS)
HEAD_DIM = 64
N_HEADS = D_MODEL // HEAD_DIM
N_KV_HEADS = N_HEADS // 4
KV_GROUP = N_HEADS // N_KV_HEADS
D_ATTN = N_HEADS * HEAD_DIM
D_KV = N_KV_HEADS * HEAD_DIM
ROPE_AXIS_DIM = HEAD_DIM // 2
ROPE_THETA = 10000.0
Q_BLOCK = 128
D_FF = 4 * D_MODEL
N_MOD = 6
EPS = 1e-6

kernel_name = "hybrid_s5_pool_gqa_dit_block"

F32 = jnp.float32


def rmsnorm(x, g):
    xf = x.astype(F32)
    y = xf * lax.rsqrt(jnp.mean(jnp.square(xf), axis=-1, keepdims=True) + EPS) * g.astype(F32)
    return y.astype(x.dtype)


def modulate(h, shift, scale):
    return h * (1.0 + scale) + shift


def sq_relu_mlp(h, w1, w2):
    return jnp.square(jax.nn.relu(h @ w1)) @ w2


def axial_rope_tables(row_ids, col_ids):
    inv_freq = ROPE_THETA ** (-jnp.arange(0, ROPE_AXIS_DIM, 2, dtype=F32) / ROPE_AXIS_DIM)
    ang_r = row_ids[:, None] * inv_freq[None, :]
    ang_c = col_ids[:, None] * inv_freq[None, :]
    return (jnp.cos(ang_r), jnp.sin(ang_r), jnp.cos(ang_c), jnp.sin(ang_c))


def rope_1d(x, cos, sin):
    half = x.shape[-1] // 2
    x1, x2 = x[..., :half], x[..., half:]
    cos = cos[None, :, None, :].astype(x.dtype)
    sin = sin[None, :, None, :].astype(x.dtype)
    return jnp.concatenate([x1 * cos - x2 * sin, x1 * sin + x2 * cos], axis=-1)


def apply_axial_rope(x, rope):
    xr = rope_1d(x[..., :ROPE_AXIS_DIM], rope[0], rope[1])
    xc = rope_1d(x[..., ROPE_AXIS_DIM:], rope[2], rope[3])
    return jnp.concatenate([xr, xc], axis=-1)


def s5_discretize(lam_re, lam_im, log_dt, b_re, b_im):
    lam_re = lam_re.astype(F32)
    lam_im = lam_im.astype(F32)
    dt = jnp.exp(log_dt.astype(F32))[:, None]
    mag = jnp.exp(lam_re * dt)
    a_re = mag * jnp.cos(lam_im * dt)
    a_im = mag * jnp.sin(lam_im * dt)
    den = lam_re * lam_re + lam_im * lam_im
    f_re = ((a_re - 1.0) * lam_re + a_im * lam_im) / den
    f_im = (a_im * lam_re - (a_re - 1.0) * lam_im) / den
    b_re = b_re.astype(F32)
    b_im = b_im.astype(F32)
    bb_re = f_re[..., None] * b_re - f_im[..., None] * b_im
    bb_im = f_re[..., None] * b_im + f_im[..., None] * b_re
    return a_re, a_im, bb_re, bb_im


def complex_affine_combine(e1, e2):
    a1r, a1i, b1r, b1i = e1
    a2r, a2i, b2r, b2i = e2
    return (a2r * a1r - a2i * a1i,
            a2r * a1i + a2i * a1r,
            a2r * b1r - a2i * b1i + b2r,
            a2r * b1i + a2i * b1r + b2i)


def diag_scan(a_re, a_im, b_re, b_im, s0_re, s0_im, reverse):
    first = -1 if reverse else 0
    b_re = b_re.at[:, first].add(a_re * s0_re - a_im * s0_im)
    b_im = b_im.at[:, first].add(a_re * s0_im + a_im * s0_re)
    A_re = jnp.broadcast_to(a_re, b_re.shape)
    A_im = jnp.broadcast_to(a_im, b_im.shape)
    _, _, h_re, h_im = lax.associative_scan(complex_affine_combine, (A_re, A_im, b_re, b_im),
                                            reverse=reverse, axis=1)
    last = 0 if reverse else -1
    return h_re, h_im, h_re[:, last], h_im[:, last]


def s5_readout(h_re, h_im, c_re, c_im):
    return (jnp.einsum('blgp,gcp->blgc', h_re, c_re.astype(F32))
            - jnp.einsum('blgp,gcp->blgc', h_im, c_im.astype(F32)))


def s5_mixer(u, uc, lam_re, lam_im, log_dt, b_re, b_im, c_re, c_im, d, glu_w, ctx_out):
    dtype = u.dtype
    bsz, n_lat = u.shape[:2]
    n_ctx = uc.shape[1]
    ug = u.astype(F32).reshape(bsz, n_lat, S5_GROUPS, S5_GROUP)
    ugc = uc.astype(F32).reshape(bsz, n_ctx, S5_GROUPS, S5_GROUP)
    dg = d.astype(F32).reshape(S5_GROUPS, S5_GROUP)
    y = ug * dg
    yc = ugc * dg
    zero = jnp.zeros((bsz, S5_GROUPS, S5_STATE), F32)
    for dr in range(2):
        rev = dr == 1
        a_re, a_im, bb_re, bb_im = s5_discretize(lam_re[dr], lam_im[dr], log_dt[dr], b_re[dr], b_im[dr])
        bc_re = jnp.einsum('blgc,gpc->blgp', ugc, bb_re)
        bc_im = jnp.einsum('blgc,gpc->blgp', ugc, bb_im)
        hc_re, hc_im, s_re, s_im = diag_scan(a_re, a_im, bc_re, bc_im, zero, zero, rev)
        bl_re = jnp.einsum('blgc,gpc->blgp', ug, bb_re)
        bl_im = jnp.einsum('blgc,gpc->blgp', ug, bb_im)
        hl_re, hl_im, _, _ = diag_scan(a_re, a_im, bl_re, bl_im, s_re, s_im, rev)
        y = y + s5_readout(hl_re, hl_im, c_re[dr], c_im[dr])
        if ctx_out:
            yc = yc + s5_readout(hc_re, hc_im, c_re[dr], c_im[dr])

    def glu(z):
        z = jax.nn.gelu(z)
        zz = z @ glu_w.astype(F32)
        return (zz[..., :D_S5] * jax.nn.sigmoid(zz[..., D_S5:])).astype(dtype)

    y_out = glu(y.reshape(bsz, n_lat, D_S5))
    yc_out = glu(yc.reshape(bsz, n_ctx, D_S5)) if ctx_out else None
    return y_out, yc_out


def centred_pool_residual(u, window):
    n = u.shape[1]
    lo = window // 2
    hi = window - 1 - lo
    csum = jnp.concatenate([jnp.zeros_like(u[:, :1]), jnp.cumsum(u, axis=1)], axis=1)
    t = jnp.arange(n)
    start = jnp.clip(t - lo, 0, n)
    end = jnp.clip(t + hi + 1, 0, n)
    count = (end - start).astype(u.dtype)
    mean = (jnp.take(csum, end, axis=1) - jnp.take(csum, start, axis=1)) / count[None, :, None]
    return mean - u


def pool_mixer(u, w, scale):
    dtype = u.dtype
    uf = u.astype(F32)
    outs = []
    for gi, window in enumerate(POOL_WINDOWS):
        seg = uf[..., gi * POOL_GROUP:(gi + 1) * POOL_GROUP]
        outs.append(centred_pool_residual(seg, window) @ w[gi].astype(F32))
    return (jnp.concatenate(outs, axis=-1) * scale.astype(F32)).astype(dtype)


def even_mixer(h, hc, w_in, w_out, lam_re, lam_im, log_dt, b_re, b_im, c_re, c_im, d, glu_w,
               pw, pscale, ctx_out):
    u = h @ w_in
    uc = hc @ w_in
    ya, yac = s5_mixer(u[..., :D_S5], uc[..., :D_S5], lam_re, lam_im, log_dt, b_re, b_im,
                       c_re, c_im, d, glu_w, ctx_out)
    yb = pool_mixer(u[..., D_S5:], pw, pscale)
    y = jnp.concatenate([ya, yb], axis=-1) @ w_out
    if ctx_out:
        ybc = pool_mixer(uc[..., D_S5:], pw, pscale)
        yc = jnp.concatenate([yac, ybc], axis=-1) @ w_out
    else:
        yc = None
    return y, yc


def gqa_attend(q, k, v):
    s = jnp.einsum('bqkgd,bskd->bkgqs', q, k, preferred_element_type=F32) * (HEAD_DIM ** -0.5)
    p = jax.nn.softmax(s, axis=-1).astype(v.dtype)
    return jnp.einsum('bkgqs,bskd->bqkgd', p, v)


def attention_mixer(h, hc, w_qkv, w_out, qn, kn, rope, ctx_out):
    bsz, n_lat = h.shape[:2]
    n_ctx = hc.shape[1]

    def project(z, n):
        qkv = z @ w_qkv
        q = rmsnorm(qkv[..., :D_ATTN].reshape(bsz, n, N_HEADS, HEAD_DIM), qn)
        k = rmsnorm(qkv[..., D_ATTN:D_ATTN + D_KV].reshape(bsz, n, N_KV_HEADS, HEAD_DIM), kn)
        v = qkv[..., D_ATTN + D_KV:].reshape(bsz, n, N_KV_HEADS, HEAD_DIM)
        return q, k, v

    q, k, v = project(h, n_lat)
    q = apply_axial_rope(q, rope)
    k = apply_axial_rope(k, rope)
    qc, kc, vc = project(hc, n_ctx)
    k_all = jnp.concatenate([kc, k], axis=1)
    v_all = jnp.concatenate([vc, v], axis=1)
    nb = n_lat // Q_BLOCK
    qb = jnp.moveaxis(q.reshape(bsz, nb, Q_BLOCK, N_KV_HEADS, KV_GROUP, HEAD_DIM), 1, 0)
    ob = lax.map(lambda qi: gqa_attend(qi, k_all, v_all), qb)
    y = jnp.moveaxis(ob, 0, 1).reshape(bsz, n_lat, D_ATTN) @ w_out
    if ctx_out:
        oc = gqa_attend(qc.reshape(bsz, n_ctx, N_KV_HEADS, KV_GROUP, HEAD_DIM), kc, vc)
        yc = oc.reshape(bsz, n_ctx, D_ATTN) @ w_out
    else:
        yc = None
    return y, yc


def setup_inputs(seed: int = 0) -> dict:
    key = jax.random.key(seed)
    ks = jax.random.split(key, 32)
    nrm = jax.random.normal
    G, P, GC = S5_GROUPS, S5_STATE, S5_GROUP
    return {
        "x": nrm(ks[0], (BATCH, SEQ, D_MODEL), F32),
        "c": nrm(ks[1], (BATCH, D_MODEL), F32),
        "ctx": nrm(ks[2], (BATCH, CTX_LEN, D_MODEL), F32),
        "c_ctx": nrm(ks[3], (D_MODEL,), F32),
        "ada_w": nrm(ks[4], (DEPTH, D_MODEL, N_MOD * D_MODEL), F32) * D_MODEL ** -0.5,
        "ada_b": nrm(ks[5], (DEPTH, N_MOD * D_MODEL), F32) * 0.01,
        "norm_g": 1.0 + 0.02 * nrm(ks[6], (DEPTH, 4, D_MODEL), F32),
        "mlp_w1": nrm(ks[7], (DEPTH, D_MODEL, D_FF), F32) * D_MODEL ** -0.5,
        "mlp_w2": nrm(ks[8], (DEPTH, D_FF, D_MODEL), F32) * D_FF ** -0.5,
        "mix_in_w": nrm(ks[9], (N_EVEN, D_MODEL, D_S5 + D_POOL), F32) * D_MODEL ** -0.5,
        "mix_out_w": nrm(ks[10], (N_EVEN, D_S5 + D_POOL, D_MODEL), F32) * (D_S5 + D_POOL) ** -0.5,
        "s5_lambda_re": -0.5 + 0.01 * nrm(ks[11], (N_EVEN, 2, G, P), F32),
        "s5_lambda_im": jnp.pi * jnp.arange(P, dtype=F32) + 0.01 * nrm(ks[12], (N_EVEN, 2, G, P), F32),
        "s5_log_dt": jax.random.uniform(ks[13], (N_EVEN, 2, G), F32, math.log(DT_MIN), math.log(DT_MAX)),
        "s5_b_re": nrm(ks[14], (N_EVEN, 2, G, P, GC), F32) * (2 * GC) ** -0.5,
        "s5_b_im": nrm(ks[15], (N_EVEN, 2, G, P, GC), F32) * (2 * GC) ** -0.5,
        "s5_c_re": nrm(ks[16], (N_EVEN, 2, G, GC, P), F32) * P ** -0.5,
        "s5_c_im": nrm(ks[17], (N_EVEN, 2, G, GC, P), F32) * P ** -0.5,
        "s5_d": nrm(ks[18], (N_EVEN, D_S5), F32),
        "s5_glu_w": nrm(ks[19], (N_EVEN, D_S5, 2 * D_S5), F32) * D_S5 ** -0.5,
        "pool_w": nrm(ks[20], (N_EVEN, len(POOL_WINDOWS), POOL_GROUP, POOL_GROUP), F32) * POOL_GROUP ** -0.5,
        "pool_scale": 1.0 + 0.1 * nrm(ks[21], (N_EVEN, D_POOL), F32),
        "attn_qkv_w": nrm(ks[22], (N_ODD, D_MODEL, D_ATTN + 2 * D_KV), F32) * D_MODEL ** -0.5,
        "attn_out_w": nrm(ks[23], (N_ODD, D_ATTN, D_MODEL), F32) * D_ATTN ** -0.5,
        "attn_q_norm": 1.0 + 0.02 * nrm(ks[24], (N_ODD, HEAD_DIM), F32),
        "attn_k_norm": 1.0 + 0.02 * nrm(ks[25], (N_ODD, HEAD_DIM), F32),
    }


def reference(x, c, ctx, c_ctx, ada_w, ada_b, norm_g, mlp_w1, mlp_w2, mix_in_w, mix_out_w,
              s5_lambda_re, s5_lambda_im, s5_log_dt, s5_b_re, s5_b_im, s5_c_re, s5_c_im, s5_d,
              s5_glu_w, pool_w, pool_scale, attn_qkv_w, attn_out_w, attn_q_norm, attn_k_norm):
    n_lat = x.shape[1]
    rows = n_lat // GRID_W
    row_ids = jnp.repeat(jnp.arange(rows), GRID_W).astype(F32)
    col_ids = jnp.tile(jnp.arange(GRID_W), rows).astype(F32)
    rope = axial_rope_tables(row_ids, col_ids)
    xc = ctx
    for l in range(DEPTH):
        last = l == DEPTH - 1
        ctx_out = not last
        mod = jax.nn.silu(c) @ ada_w[l] + ada_b[l]
        modc = jax.nn.silu(c_ctx) @ ada_w[l] + ada_b[l]
        sh_m, sc_m, g_m, sh_f, sc_f, g_f = jnp.split(mod[:, None, :], N_MOD, axis=-1)
        csh_m, csc_m, cg_m, csh_f, csc_f, cg_f = jnp.split(modc, N_MOD, axis=-1)
        h = modulate(rmsnorm(x, norm_g[l, 0]), sh_m, sc_m)
        hc = modulate(rmsnorm(xc, norm_g[l, 0]), csh_m, csc_m)
        if l % 2 == 0:
            e = l // 2
            y, yc = even_mixer(h, hc, mix_in_w[e], mix_out_w[e], s5_lambda_re[e], s5_lambda_im[e],
                               s5_log_dt[e], s5_b_re[e], s5_b_im[e], s5_c_re[e], s5_c_im[e], s5_d[e],
                               s5_glu_w[e], pool_w[e], pool_scale[e], ctx_out)
        else:
            o = l // 2
            y, yc = attention_mixer(h, hc, attn_qkv_w[o], attn_out_w[o], attn_q_norm[o],
                                    attn_k_norm[o], rope, ctx_out)
        x = x + g_m * rmsnorm(y, norm_g[l, 1])
        hf = modulate(rmsnorm(x, norm_g[l, 2]), sh_f, sc_f)
        x = x + g_f * rmsnorm(sq_relu_mlp(hf, mlp_w1[l], mlp_w2[l]), norm_g[l, 3])
        if ctx_out:
            xc = xc + cg_m * rmsnorm(yc, norm_g[l, 1])
            hfc = modulate(rmsnorm(xc, norm_g[l, 2]), csh_f, csc_f)
            xc = xc + cg_f * rmsnorm(sq_relu_mlp(hfc, mlp_w1[l], mlp_w2[l]), norm_g[l, 3])
    return x
```

```python
import functools
import math

import jax
import jax.numpy as jnp
from jax import lax
from jax.experimental import pallas as pl
from jax.experimental.pallas import tpu as pltpu

F32 = jnp.float32
BF16 = jnp.bfloat16

EPS = 1e-6
N_MOD = 6
S5_GROUP = 16
S5_STATE = 64
POOL_WINDOWS = (2, 4, 8, 16)
HEAD_DIM = 64
KV_GROUP = 4
GRID_W = 64
ROPE_THETA = 10000.0

LANES = 128
SUBLANES = 8
VMEM_LIMIT = 56 * 1024 * 1024

CHUNK = 256
SUB = CHUNK // SUBLANES
POOL_HALO = 8


def _cparams(sem):
    return pltpu.CompilerParams(dimension_semantics=sem, vmem_limit_bytes=VMEM_LIMIT)


def _rms(x, g):
    ms = jnp.mean(x * x, axis=-1, keepdims=True)
    return x * lax.rsqrt(ms + EPS) * g


def _full(shape):
    n = len(shape)
    return pl.BlockSpec(shape, lambda *_: (0,) * n, pipeline_mode=pl.Buffered(1))


def _mod_kernel(cb_ref, w_ref, b_ref, o_ref):
    nb = w_ref.shape[2]
    rows = []
    for r in range(2):
        s = jax.nn.silu(cb_ref[r])
        cols = []
        for j in range(nb // LANES):
            w = w_ref[0, :, j * LANES:(j + 1) * LANES]
            cols.append(jnp.sum(w * s, axis=0, keepdims=True))
        rows.append(jnp.concatenate(cols, axis=1) + b_ref[0])
    rows.append(jnp.zeros((SUBLANES - 2, nb), F32))
    o_ref[0] = jnp.concatenate(rows, axis=0)


def _modulation(c, c_ctx, ada_w, ada_b):
    depth, d, n = ada_w.shape
    nb = 1024
    cb = jnp.broadcast_to(jnp.stack([c[0], c_ctx])[:, :, None], (2, d, LANES))
    out = pl.pallas_call(
        _mod_kernel,
        grid=(depth, n // nb),
        in_specs=[
            pl.BlockSpec((2, d, LANES), lambda l, j: (0, 0, 0)),
            pl.BlockSpec((1, d, nb), lambda l, j: (l, 0, j)),
            pl.BlockSpec((1, 1, nb), lambda l, j: (l, 0, j)),
        ],
        out_specs=pl.BlockSpec((1, SUBLANES, nb), lambda l, j: (l, 0, j)),
        out_shape=jax.ShapeDtypeStruct((depth, SUBLANES, n), F32),
        compiler_params=_cparams(("arbitrary", "arbitrary")),
        name="adaln_mod",
    )(cb, ada_w, ada_b.reshape(depth, 1, n))
    mod = out[:, :2].reshape(depth, 2, N_MOD, d)
    return jnp.pad(mod, ((0, 0), (0, 0), (0, SUBLANES - N_MOD), (0, 0)))


def _inproj_kernel(x_ref, mod_ref, g_ref, w_ref, u_ref):
    h = _rms(x_ref[...], g_ref[...]) * (1.0 + mod_ref[1:2, :]) + mod_ref[0:1, :]
    u_ref[...] = jnp.dot(h.astype(BF16), w_ref[...], preferred_element_type=F32)


def _inproj(x, mod, g, w, tm):
    n, d = x.shape
    return pl.pallas_call(
        _inproj_kernel,
        grid=(n // tm,),
        in_specs=[pl.BlockSpec((tm, d), lambda i: (i, 0)), _full(mod.shape), _full(g.shape), _full(w.shape)],
        out_specs=pl.BlockSpec((tm, w.shape[1]), lambda i: (i, 0)),
        out_shape=jax.ShapeDtypeStruct((n, w.shape[1]), F32),
        compiler_params=_cparams(("arbitrary",)),
        name="inproj",
    )(x, mod, g, w)


def _s5prep_kernel(lre_ref, lim_ref, ldt_ref, bre_ref, bim_ref,
                   are_o, aim_o, bbre_o, bbim_o, pre_o, pim_o):
    lre = lre_ref[0]
    lim = lim_ref[0]
    dt = jnp.exp(ldt_ref[0])
    mag = jnp.exp(lre * dt)
    are = mag * jnp.cos(lim * dt)
    aim = mag * jnp.sin(lim * dt)
    den = lre * lre + lim * lim
    fre = ((are - 1.0) * lre + aim * lim) / den
    fim = (aim * lre - (are - 1.0) * lim) / den
    bre = bre_ref[0]
    bim = bim_ref[0]
    are_o[0] = are
    aim_o[0] = aim
    bbre_o[0] = fre * bre - fim * bim
    bbim_o[0] = fre * bim + fim * bre
    pr, pi = are, aim
    for i in range(SUB):
        pre_o[0, i:i + 1, :] = pr
        pim_o[0, i:i + 1, :] = pi
        pr, pi = pr * are - pi * aim, pr * aim + pi * are


def _s5_prepare(lam_re, lam_im, log_dt, b_re, b_im, c_re, c_im):
    nd, g, p = lam_re.shape
    gc = b_re.shape[-1]
    s = g * p
    lre = lam_re.reshape(nd, 1, s)
    lim = lam_im.reshape(nd, 1, s)
    ldt = jnp.repeat(log_dt, p, axis=-1).reshape(nd, 1, s)
    bre = jnp.transpose(b_re, (0, 3, 1, 2)).reshape(nd, gc, s)
    bim = jnp.transpose(b_im, (0, 3, 1, 2)).reshape(nd, gc, s)
    vec = pl.BlockSpec((1, 1, s), lambda d: (d, 0, 0))
    mat = pl.BlockSpec((1, gc, s), lambda d: (d, 0, 0))
    pw = pl.BlockSpec((1, SUB, s), lambda d: (d, 0, 0))
    are, aim, bbre, bbim, pre, pim = pl.pallas_call(
        _s5prep_kernel,
        grid=(nd,),
        in_specs=[vec, vec, vec, mat, mat],
        out_specs=[vec, vec, mat, mat, pw, pw],
        out_shape=[jax.ShapeDtypeStruct((nd, 1, s), F32)] * 2
        + [jax.ShapeDtypeStruct((nd, gc, s), F32)] * 2
        + [jax.ShapeDtypeStruct((nd, SUB, s), F32)] * 2,
        compiler_params=_cparams(("arbitrary",)),
        name="s5_discretize",
    )(lre, lim, ldt, bre, bim)
    a8 = jnp.broadcast_to(jnp.stack([are, aim], axis=1), (nd, 2, SUBLANES, s))
    apow = jnp.stack([pre, pim], axis=1)
    gps = LANES // gc
    nslab = g // gps
    eye = jnp.eye(gps, dtype=F32)

    def in_slabs(bb):
        t = bb.reshape(nd, gc, nslab, gps, p)
        t = jnp.einsum('dcsgp,hg->dshcgp', t, eye)
        return t.reshape(nd, nslab, LANES, gps * p)

    def out_slabs(cc):
        t = cc.reshape(nd, nslab, gps, gc, p)
        t = jnp.einsum('dsgcp,hg->dsgphc', t, eye)
        return t.reshape(nd, nslab, gps * p, LANES)

    bmat = jnp.stack([in_slabs(bbre), in_slabs(bbim)], axis=1).astype(BF16)
    cmat = jnp.stack([out_slabs(c_re), out_slabs(-c_im)], axis=1).astype(BF16)
    return a8, apow, bmat, cmat


def _scan_kernel(uc_ref, ulf_ref, ulb_ref, a8_ref, ap_ref, b_ref, c_ref,
                 yfc_ref, yfl_ref, ybc_ref, ybl_ref,
                 uin_scr, up_scr, hre, him, carry, yp_scr, yout_scr):
    k = pl.program_id(0)
    nslab = up_scr.shape[0]
    s_tot = hre.shape[1]
    cw = s_tot // nslab
    groups = CHUNK // SUBLANES

    @pl.when(k == 0)
    def _():
        carry[...] = jnp.zeros(carry.shape, F32)

    def run_dir(d, y_lat_ref, y_ctx_ref):
        rev = d == 1
        for r in range(groups):
            j, i0 = divmod(r * SUBLANES, SUB)
            for s in range(nslab):
                up_scr[s, pl.ds(i0 * SUBLANES + j, SUBLANES, stride=SUBLANES), :] = (
                    uin_scr[d, r * SUBLANES:(r + 1) * SUBLANES, s * LANES:(s + 1) * LANES])
        for s in range(nslab):
            ub = up_scr[s].astype(BF16)
            hre[:, s * cw:(s + 1) * cw] = jnp.dot(ub, b_ref[d, 0, s], preferred_element_type=F32)
            him[:, s * cw:(s + 1) * cw] = jnp.dot(ub, b_ref[d, 1, s], preferred_element_type=F32)
        for s in range(nslab):
            cols = pl.ds(s * cw, cw)
            are = a8_ref[d, 0, :, s * cw:(s + 1) * cw]
            aim = a8_ref[d, 1, :, s * cw:(s + 1) * cw]

            def step(ii, hc):
                hr, hi = hc
                i = (SUB - 1 - ii) if rev else ii
                rows = pl.ds(pl.multiple_of(i * SUBLANES, SUBLANES), SUBLANES)
                nr = are * hr - aim * hi + hre[rows, cols]
                ni = are * hi + aim * hr + him[rows, cols]
                hre[rows, cols] = nr
                him[rows, cols] = ni
                return nr, ni

            z = jnp.zeros((SUBLANES, cw), F32)
            lax.fori_loop(0, SUB, step, (z, z))
        end = 0 if rev else (SUB - 1) * SUBLANES
        ere = hre[end:end + SUBLANES, :]
        eim = him[end:end + SUBLANES, :]
        ar = ap_ref[d, 0, SUB - 1:SUB, :]
        ai = ap_ref[d, 1, SUB - 1:SUB, :]
        sr = carry[d, 0, 0:1, :]
        si = carry[d, 1, 0:1, :]
        srows, sirows = [], []
        order = range(SUBLANES - 1, -1, -1) if rev else range(SUBLANES)
        for j in order:
            srows.append(sr)
            sirows.append(si)
            sr, si = (ar * sr - ai * si + ere[j:j + 1, :], ar * si + ai * sr + eim[j:j + 1, :])
        carry[d, 0, 0:1, :] = sr
        carry[d, 1, 0:1, :] = si
        if rev:
            srows.reverse()
            sirows.reverse()
        sin_re = jnp.concatenate(srows, axis=0)
        sin_im = jnp.concatenate(sirows, axis=0)
        for s in range(nslab):
            cols = pl.ds(s * cw, cw)
            s_r = sin_re[:, s * cw:(s + 1) * cw]
            s_i = sin_im[:, s * cw:(s + 1) * cw]

            def fix(i, _):
                pidx = (SUB - 1 - i) if rev else i
                rows = pl.ds(pl.multiple_of(i * SUBLANES, SUBLANES), SUBLANES)
                pr = ap_ref[d, 0, pl.ds(pidx, 1), cols]
                pi = ap_ref[d, 1, pl.ds(pidx, 1), cols]
                hre[rows, cols] = hre[rows, cols] + (pr * s_r - pi * s_i)
                him[rows, cols] = him[rows, cols] + (pr * s_i + pi * s_r)
                return 0

            lax.fori_loop(0, SUB, fix, 0)
        for s in range(nslab):
            yp_scr[s] = (
                jnp.dot(hre[:, s * cw:(s + 1) * cw].astype(BF16), c_ref[d, 0, s], preferred_element_type=F32)
                + jnp.dot(him[:, s * cw:(s + 1) * cw].astype(BF16), c_ref[d, 1, s], preferred_element_type=F32))
        for r in range(groups):
            j, i0 = divmod(r * SUBLANES, SUB)
            for s in range(nslab):
                yout_scr[r * SUBLANES:(r + 1) * SUBLANES, s * LANES:(s + 1) * LANES] = (
                    yp_scr[s, pl.ds(i0 * SUBLANES + j, SUBLANES, stride=SUBLANES), :])
        y_lat_ref[...] = yout_scr[...]

        @pl.when(k == 0)
        def _():
            y_ctx_ref[...] = yout_scr[...]

    @pl.when(k == 0)
    def _():
        uin_scr[0] = uc_ref[...]
        uin_scr[1] = uc_ref[...]

    @pl.when(k > 0)
    def _():
        uin_scr[0] = ulf_ref[...]
        uin_scr[1] = ulb_ref[...]

    run_dir(0, yfl_ref, yfc_ref)
    run_dir(1, ybl_ref, ybc_ref)


def _s5_scan(u_ctx, u_lat, a8, apow, bmat, cmat):
    n_lat = u_lat.shape[0]
    nl = n_lat // CHUNK
    ds5 = bmat.shape[2] * LANES
    s_tot = a8.shape[-1]
    nslab = bmat.shape[2]
    fwd = lambda k: (jnp.maximum(k - 1, 0), 0)
    bwd = lambda k: (jnp.where(k == 0, nl - 1, nl - k), 0)
    zero = lambda k: (0, 0)
    blk = (CHUNK, ds5)
    return pl.pallas_call(
        _scan_kernel,
        grid=(nl + 1,),
        in_specs=[pl.BlockSpec(blk, zero), pl.BlockSpec(blk, fwd), pl.BlockSpec(blk, bwd),
                  _full(a8.shape), _full(apow.shape), _full(bmat.shape), _full(cmat.shape)],
        out_specs=[pl.BlockSpec(blk, zero), pl.BlockSpec(blk, fwd),
                   pl.BlockSpec(blk, zero), pl.BlockSpec(blk, bwd)],
        out_shape=[jax.ShapeDtypeStruct((CHUNK, ds5), F32), jax.ShapeDtypeStruct((n_lat, ds5), F32),
                   jax.ShapeDtypeStruct((CHUNK, ds5), F32), jax.ShapeDtypeStruct((n_lat, ds5), F32)],
        scratch_shapes=[
            pltpu.VMEM((2, CHUNK, ds5), F32),
            pltpu.VMEM((nslab, CHUNK, LANES), F32),
            pltpu.VMEM((CHUNK, s_tot), F32),
            pltpu.VMEM((CHUNK, s_tot), F32),
            pltpu.VMEM((2, 2, SUBLANES, s_tot), F32),
            pltpu.VMEM((nslab, CHUNK, LANES), F32),
            pltpu.VMEM((CHUNK, ds5), F32),
        ],
        compiler_params=_cparams(("arbitrary",)),
        name="s5_scan",
    )(u_ctx, u_lat, u_lat, a8, apow, bmat, cmat)


def _split3(x):
    hi = x.astype(BF16)
    r1 = x - hi.astype(F32)
    mid = r1.astype(BF16)
    lo = (r1 - mid.astype(F32)).astype(BF16)
    return hi, mid, lo


def _mixout_kernel(x_ref, u_ref, yf_ref, yb_ref, prev_ref, next_ref, mod_ref, g_ref, d_ref,
                   glu_ref, pw_ref, ps_ref, wo_ref, o_ref, *, nblk):
    i = pl.program_id(0)
    tm = x_ref.shape[0]
    ds5 = yf_ref.shape[1]
    u = u_ref[...]
    y = u[:, :ds5] * d_ref[...] + yf_ref[...] + yb_ref[...]
    zz = jnp.dot(jax.nn.gelu(y).astype(BF16), glu_ref[...], preferred_element_type=F32)
    ya = zz[:, :ds5] * jax.nn.sigmoid(zz[:, ds5:])
    ext_rows = tm + LANES
    diff = (lax.broadcasted_iota(jnp.int32, (tm, ext_rows), 1) - POOL_HALO
            - lax.broadcasted_iota(jnp.int32, (tm, ext_rows), 0))
    tcol = lax.broadcasted_iota(jnp.int32, (tm, 1), 0)
    erow = lax.broadcasted_iota(jnp.int32, (ext_rows, 1), 0) - POOL_HALO
    lo_lim = jnp.where(i > 0, -POOL_HALO, 0)
    hi_lim = jnp.where(i < nblk - 1, tm + POOL_HALO - 1, tm - 1)
    row_ok = (erow >= lo_lim) & (erow <= hi_lim)
    pg = pw_ref.shape[1]
    outs = []
    for gi, window in enumerate(POOL_WINDOWS):
        lo = window // 2
        hi = window - 1 - lo
        seg = u[:, ds5 + gi * pg:ds5 + (gi + 1) * pg]
        ext = jnp.concatenate(
            [prev_ref[:, gi * pg:(gi + 1) * pg], seg, next_ref[:, gi * pg:(gi + 1) * pg],
             jnp.zeros((ext_rows - tm - 2 * POOL_HALO, pg), F32)], axis=0)
        ext = jnp.where(row_ok, ext, 0.0)
        band = jnp.where((diff >= -lo) & (diff <= hi), 1.0, 0.0).astype(BF16)
        wsum = sum(jnp.dot(band, part, preferred_element_type=F32) for part in _split3(ext))
        count = (jnp.minimum(tcol + hi, hi_lim) - jnp.maximum(tcol - lo, lo_lim) + 1).astype(F32)
        resid = wsum / count - seg
        outs.append(jnp.dot(resid.astype(BF16), pw_ref[gi], preferred_element_type=F32))
    ybp = jnp.concatenate(outs, axis=1) * ps_ref[...]
    cat = jnp.concatenate([ya, ybp], axis=1).astype(BF16)
    mix = jnp.dot(cat, wo_ref[...], preferred_element_type=F32)
    o_ref[...] = x_ref[...] + mod_ref[2:3, :] * _rms(mix, g_ref[...])


def _mixout(x, u, yf, yb, mod, g, d, glu_w, pool_w, pool_scale, w_out, tm):
    n, dm = x.shape
    ds5 = yf.shape[1]
    nblk = n // tm
    hb = tm // POOL_HALO
    nhalo = n // POOL_HALO
    prev_map = lambda i: (jnp.maximum(i * hb - 1, 0), 1)
    next_map = lambda i: (jnp.minimum((i + 1) * hb, nhalo - 1), 1)
    row = lambda i: (i, 0)
    return pl.pallas_call(
        functools.partial(_mixout_kernel, nblk=nblk),
        grid=(nblk,),
        in_specs=[pl.BlockSpec((tm, dm), row), pl.BlockSpec((tm, dm), row),
                  pl.BlockSpec((tm, ds5), row), pl.BlockSpec((tm, ds5), row),
                  pl.BlockSpec((POOL_HALO, dm - ds5), prev_map), pl.BlockSpec((POOL_HALO, dm - ds5), next_map),
                  _full(mod.shape), _full(g.shape), _full(d.shape), _full(glu_w.shape),
                  _full(pool_w.shape), _full(pool_scale.shape), _full(w_out.shape)],
        out_specs=pl.BlockSpec((tm, dm), row),
        out_shape=jax.ShapeDtypeStruct((n, dm), F32),
        compiler_params=_cparams(("arbitrary",)),
        name="mixout",
    )(x, u, yf, yb, u, u, mod, g, d, glu_w, pool_w, pool_scale, w_out)


def _mlp_tail(x1, mod_ref, g2, g3, w1_ref, w2_ref):
    dff = w1_ref.shape[1]
    fc = 1024
    hb = (_rms(x1, g2) * (1.0 + mod_ref[4:5, :]) + mod_ref[3:4, :]).astype(BF16)
    acc = jnp.zeros(x1.shape, F32)
    for c in range(dff // fc):
        h1 = jnp.dot(hb, w1_ref[:, c * fc:(c + 1) * fc], preferred_element_type=F32)
        h1 = jnp.square(jnp.maximum(h1, 0.0)).astype(BF16)
        acc = acc + jnp.dot(h1, w2_ref[c * fc:(c + 1) * fc, :], preferred_element_type=F32)
    return x1 + mod_ref[5:6, :] * _rms(acc, g3)


def _mlp_kernel(x_ref, mod_ref, g_ref, w1_ref, w2_ref, o_ref):
    o_ref[...] = _mlp_tail(x_ref[...], mod_ref, g_ref[2:3, :], g_ref[3:4, :], w1_ref, w2_ref)


def _attnout_mlp_kernel(x_ref, a_ref, wo_ref, mod_ref, g_ref, w1_ref, w2_ref, o_ref):
    y = jnp.dot(a_ref[...], wo_ref[...], preferred_element_type=F32)
    x1 = x_ref[...] + mod_ref[2:3, :] * _rms(y, g_ref[1:2, :])
    o_ref[...] = _mlp_tail(x1, mod_ref, g_ref[2:3, :], g_ref[3:4, :], w1_ref, w2_ref)


def _mlp(x, mod, g4, w1, w2, tm, attn=None, w_out=None):
    n, d = x.shape
    row = lambda i: (i, 0)
    if attn is None:
        kern = _mlp_kernel
        args = (x, mod, g4, w1, w2)
        specs = [pl.BlockSpec((tm, d), row), _full(mod.shape), _full(g4.shape), _full(w1.shape), _full(w2.shape)]
    else:
        kern = _attnout_mlp_kernel
        args = (x, attn, w_out, mod, g4, w1, w2)
        specs = [pl.BlockSpec((tm, d), row), pl.BlockSpec((tm, attn.shape[1]), row), _full(w_out.shape),
                 _full(mod.shape), _full(g4.shape), _full(w1.shape), _full(w2.shape)]
    return pl.pallas_call(
        kern,
        grid=(n // tm,),
        in_specs=specs,
        out_specs=pl.BlockSpec((tm, d), row),
        out_shape=jax.ShapeDtypeStruct((n, d), F32),
        compiler_params=_cparams(("arbitrary",)),
        name="mlp" if attn is None else "attnout_mlp",
    )(*args)


def _swap16(x):
    lane = lax.broadcasted_iota(jnp.int32, x.shape, 1)
    return jnp.where((lane % 32) < 16, pltpu.roll(x, LANES - 16, 1), pltpu.roll(x, 16, 1))


def _head_rms(x, ones_bd, gain):
    sq = x * x
    hi = sq.astype(BF16)
    lo = (sq - hi.astype(F32)).astype(BF16)
    ssq = jnp.dot(hi, ones_bd, preferred_element_type=F32) + jnp.dot(lo, ones_bd, preferred_element_type=F32)
    return x * lax.rsqrt(ssq * (1.0 / HEAD_DIM) + EPS) * gain


def _rope(x, cos, sin):
    tiles = []
    for j in range(x.shape[1] // LANES):
        xt = x[:, j * LANES:(j + 1) * LANES]
        tiles.append(xt * cos + _swap16(xt) * sin)
    return jnp.concatenate(tiles, axis=1)


def _qkv_kernel(x_ref, mod_ref, g_ref, w_ref, qn_ref, kn_ref, cos_ref, sin_ref, oq_ref, ok_ref, dup_ref, dupt_ref,
                q_ref, kt_ref, v_ref, *, rope):
    dq = oq_ref.shape[0]
    dkv = ok_ref.shape[0]
    h = _rms(x_ref[...], g_ref[...]) * (1.0 + mod_ref[1:2, :]) + mod_ref[0:1, :]
    qkv = jnp.dot(h.astype(BF16), w_ref[...], preferred_element_type=F32)
    q = _head_rms(qkv[:, :dq], oq_ref[...], qn_ref[...])
    k = _head_rms(qkv[:, dq:dq + dkv], ok_ref[...], kn_ref[...])
    v = qkv[:, dq + dkv:]
    if rope:
        q = _rope(q, cos_ref[...], sin_ref[...])
        k = _rope(k, cos_ref[...], sin_ref[...])
    q_ref[...] = (q * (HEAD_DIM ** -0.5)).astype(BF16)
    kt = lax.dot_general(dupt_ref[...], k.astype(BF16), (((1,), (1,)), ((), ())), preferred_element_type=F32)
    vv = jnp.dot(v.astype(BF16), dup_ref[...], preferred_element_type=F32)
    wide = KV_GROUP * HEAD_DIM
    for g in range(kt_ref.shape[0]):
        kt_ref[g] = kt[g * wide:(g + 1) * wide, :].astype(BF16)
        v_ref[g] = vv[:, g * wide:(g + 1) * wide].astype(BF16)


def _qkv(x, mod, g, w, qn, kn, cos, sin, consts, tm, rope):
    n, d = x.shape
    ones_q, ones_k, dup, dupt = consts
    dq = ones_q.shape[0]
    dkv = ones_k.shape[0]
    nkv = dkv // HEAD_DIM
    wide = KV_GROUP * HEAD_DIM
    row = lambda i: (i, 0)
    return pl.pallas_call(
        functools.partial(_qkv_kernel, rope=rope),
        grid=(n // tm,),
        in_specs=[pl.BlockSpec((tm, d), row), _full(mod.shape), _full(g.shape), _full(w.shape),
                  _full(qn.shape), _full(kn.shape),
                  pl.BlockSpec((tm, LANES), row), pl.BlockSpec((tm, LANES), row),
                  _full(ones_q.shape), _full(ones_k.shape), _full(dup.shape), _full(dupt.shape)],
        out_specs=[pl.BlockSpec((tm, dq), row),
                   pl.BlockSpec((nkv, wide, tm), lambda i: (0, 0, i)),
                   pl.BlockSpec((nkv, tm, wide), lambda i: (0, i, 0))],
        out_shape=[jax.ShapeDtypeStruct((n, dq), BF16),
                   jax.ShapeDtypeStruct((nkv, wide, n), BF16),
                   jax.ShapeDtypeStruct((nkv, n, wide), BF16)],
        compiler_params=_cparams(("arbitrary",)),
        name="qkv_rope" if rope else "qkv_ctx",
    )(x, mod, g, w, qn, kn, cos, sin, ones_q, ones_k, dup, dupt)


def _qkv_constants(dq, dkv):
    def ones_bd(n):
        idx = jnp.arange(n) // HEAD_DIM
        return (idx[:, None] == idx[None, :]).astype(BF16)
    wide = KV_GROUP * HEAD_DIM
    src = jnp.arange(dkv)
    dst = jnp.arange(dkv * KV_GROUP)
    dup = (src[:, None] == ((dst // wide) * HEAD_DIM + dst % HEAD_DIM)[None, :]).astype(BF16)
    return ones_bd(dq), ones_bd(dkv), dup, dup.T


def _rope_tables(n_lat):
    rows = n_lat // GRID_W
    row_ids = jnp.repeat(jnp.arange(rows), GRID_W).astype(F32)
    col_ids = jnp.tile(jnp.arange(GRID_W), rows).astype(F32)
    axis_dim = HEAD_DIM // 2
    inv_freq = ROPE_THETA ** (-jnp.arange(0, axis_dim, 2, dtype=F32) / axis_dim)
    ang_r = row_ids[:, None] * inv_freq[None, :]
    ang_c = col_ids[:, None] * inv_freq[None, :]
    cos = jnp.concatenate([jnp.cos(ang_r)] * 2 + [jnp.cos(ang_c)] * 2, axis=1)
    sin = jnp.concatenate([-jnp.sin(ang_r), jnp.sin(ang_r), -jnp.sin(ang_c), jnp.sin(ang_c)], axis=1)
    reps = LANES // HEAD_DIM
    return jnp.tile(cos, (1, reps)), jnp.tile(sin, (1, reps))


def _attn_kernel(q_ref, ktc_ref, vc_ref, ktl_ref, vl_ref, o_ref, qm_scr, m_scr, l_scr, acc_scr, *, tk):
    tq, wide = q_ref.shape
    n_lat = ktl_ref.shape[2]
    lane = lax.broadcasted_iota(jnp.int32, (1, wide), 1) // HEAD_DIM
    q = q_ref[...]
    for h in range(KV_GROUP):
        qm_scr[h] = jnp.where(lane == h, q, jnp.zeros_like(q))
    m_scr[...] = jnp.full(m_scr.shape, -jnp.inf, F32)
    l_scr[...] = jnp.zeros(l_scr.shape, F32)
    acc_scr[...] = jnp.zeros(acc_scr.shape, F32)

    def block(kt, v):
        pv = jnp.zeros((tq, wide), F32)
        alpha_w = jnp.zeros((tq, wide), F32)
        for h in range(KV_GROUP):
            s = jnp.dot(qm_scr[h], kt, preferred_element_type=F32)
            m_old = m_scr[h]
            m_new = jnp.maximum(m_old, jnp.max(s, axis=1, keepdims=True))
            alpha = jnp.exp(m_old - m_new)
            p = jnp.exp(s - m_new)
            l_scr[h] = alpha * l_scr[h] + jnp.sum(p, axis=1, keepdims=True)
            m_scr[h] = m_new
            vm = jnp.where(lane.reshape(1, wide) == h, v, jnp.zeros_like(v))
            pv = pv + jnp.dot(p.astype(BF16), vm, preferred_element_type=F32)
            alpha_w = jnp.where(lane == h, alpha, alpha_w)
        acc_scr[...] = acc_scr[...] * alpha_w + pv

    block(ktc_ref[0], vc_ref[0])

    def body(j, _):
        start = pl.multiple_of(j * tk, tk)
        block(ktl_ref[0, :, pl.ds(start, tk)], vl_ref[0, pl.ds(start, tk), :])
        return 0

    lax.fori_loop(0, n_lat // tk, body, 0)
    l_w = jnp.zeros((tq, wide), F32)
    for h in range(KV_GROUP):
        l_w = jnp.where(lane == h, l_scr[h], l_w)
    o_ref[...] = (acc_scr[...] / l_w).astype(o_ref.dtype)


def _attention(q, kt_ctx, v_ctx, kt_lat, v_lat, tq, tk):
    n, dq = q.shape
    nkv, wide, n_ctx = kt_ctx.shape
    return pl.pallas_call(
        functools.partial(_attn_kernel, tk=tk),
        grid=(nkv, n // tq),
        in_specs=[pl.BlockSpec((tq, wide), lambda g, i: (i, g)),
                  pl.BlockSpec((1, wide, n_ctx), lambda g, i: (g, 0, 0)),
                  pl.BlockSpec((1, n_ctx, wide), lambda g, i: (g, 0, 0)),
                  pl.BlockSpec((1, wide, n), lambda g, i: (g, 0, 0)),
                  pl.BlockSpec((1, n, wide), lambda g, i: (g, 0, 0))],
        out_specs=pl.BlockSpec((tq, wide), lambda g, i: (i, g)),
        out_shape=jax.ShapeDtypeStruct((n, dq), BF16),
        scratch_shapes=[pltpu.VMEM((KV_GROUP, tq, wide), BF16),
                        pltpu.VMEM((KV_GROUP, tq, 1), F32),
                        pltpu.VMEM((KV_GROUP, tq, 1), F32),
                        pltpu.VMEM((tq, wide), F32)],
        compiler_params=_cparams(("arbitrary", "arbitrary")),
        name="gqa_flash",
    )(q, kt_ctx, v_ctx, kt_lat, v_lat)


def kernel(x, c, ctx, c_ctx, ada_w, ada_b, norm_g, mlp_w1, mlp_w2, mix_in_w, mix_out_w, s5_lambda_re, s5_lambda_im, s5_log_dt, s5_b_re, s5_b_im, s5_c_re, s5_c_im, s5_d, s5_glu_w, pool_w, pool_scale, attn_qkv_w, attn_out_w, attn_q_norm, attn_k_norm):
    assert x.shape[0] == 1 and ctx.shape[0] == 1
    xl = x[0]
    xc = ctx[0]
    n_lat, d = xl.shape
    n_ctx = xc.shape[0]
    assert n_ctx == CHUNK and n_lat % 1024 == 0 and n_lat % GRID_W == 0
    tm = 512

    mod = _modulation(c, c_ctx, ada_w, ada_b)
    w1 = mlp_w1.astype(BF16)
    w2 = mlp_w2.astype(BF16)

    g0 = norm_g[0]
    w_in = mix_in_w[0].astype(BF16)
    u_lat = _inproj(xl, mod[0, 0], g0[0:1], w_in, tm)
    u_ctx = _inproj(xc, mod[0, 1], g0[0:1], w_in, n_ctx)
    a8, apow, bmat, cmat = _s5_prepare(s5_lambda_re[0], s5_lambda_im[0], s5_log_dt[0], s5_b_re[0], s5_b_im[0],
                                       s5_c_re[0], s5_c_im[0])
    yf_ctx, yf_lat, yb_ctx, yb_lat = _s5_scan(u_ctx, u_lat, a8, apow, bmat, cmat)
    mix_args = (s5_d[0].reshape(1, -1), s5_glu_w[0].astype(BF16), pool_w[0].astype(BF16),
                pool_scale[0].reshape(1, -1), mix_out_w[0].astype(BF16))
    xl = _mixout(xl, u_lat, yf_lat, yb_lat, mod[0, 0], g0[1:2], *mix_args, tm=CHUNK)
    xc = _mixout(xc, u_ctx, yf_ctx, yb_ctx, mod[0, 1], g0[1:2], *mix_args, tm=CHUNK)
    xl = _mlp(xl, mod[0, 0], g0, w1[0], w2[0], tm)
    xc = _mlp(xc, mod[0, 1], g0, w1[0], w2[0], n_ctx)

    g1 = norm_g[1]
    w_qkv = attn_qkv_w[0].astype(BF16)
    dkv = (w_qkv.shape[1] - d) // 2
    consts = _qkv_constants(d, dkv)
    reps = d // HEAD_DIM
    qn = jnp.tile(attn_q_norm[0], reps).reshape(1, -1)
    kn = jnp.tile(attn_k_norm[0], dkv // HEAD_DIM).reshape(1, -1)
    cos, sin = _rope_tables(n_lat)
    q, kt_lat, v_lat = _qkv(xl, mod[1, 0], g1[0:1], w_qkv, qn, kn, cos, sin, consts, tm, rope=True)
    _, kt_ctx, v_ctx = _qkv(xc, mod[1, 1], g1[0:1], w_qkv, qn, kn, cos[:n_ctx], sin[:n_ctx], consts, n_ctx, rope=False)
    att = _attention(q, kt_ctx, v_ctx, kt_lat, v_lat, tq=512, tk=1024)
    xl = _mlp(xl, mod[1, 0], g1, w1[1], w2[1], tm, attn=att, w_out=attn_out_w[0].astype(BF16))
    return xl[None]
```

```python
import functools
import math

import jax
import jax.numpy as jnp
from jax import lax
from jax.experimental import pallas as pl
from jax.experimental.pallas import tpu as pltpu

F32 = jnp.float32
BF16 = jnp.bfloat16

EPS = 1e-6
N_MOD = 6
S5_GROUP = 16
S5_STATE = 64
POOL_WINDOWS = (2, 4, 8, 16)
HEAD_DIM = 64
KV_GROUP = 4
GRID_W = 64
ROPE_THETA = 10000.0
LOG2E = math.log2(math.e)
SAFE_EXP2_BOUND = 60.0

LANES = 128
SUBLANES = 8
VMEM_LIMIT = 56 * 1024 * 1024

CHUNK = 256
SUB = CHUNK // SUBLANES
POOL_HALO = 8


def _cparams(sem):
    return pltpu.CompilerParams(dimension_semantics=sem, vmem_limit_bytes=VMEM_LIMIT)


def _rms(x, g):
    ms = jnp.mean(x * x, axis=-1, keepdims=True)
    return x * lax.rsqrt(ms + EPS) * g


def _full(shape):
    n = len(shape)
    return pl.BlockSpec(shape, lambda *_: (0,) * n, pipeline_mode=pl.Buffered(1))


def _mod_kernel(cb_ref, w_ref, b_ref, o_ref):
    nb = w_ref.shape[2]
    rows = []
    for r in range(2):
        s = jax.nn.silu(cb_ref[r])
        cols = []
        for j in range(nb // LANES):
            w = w_ref[0, :, j * LANES:(j + 1) * LANES]
            cols.append(jnp.sum(w * s, axis=0, keepdims=True))
        rows.append(jnp.concatenate(cols, axis=1) + b_ref[0])
    rows.append(jnp.zeros((SUBLANES - 2, nb), F32))
    o_ref[0] = jnp.concatenate(rows, axis=0)


def _modulation(c, c_ctx, ada_w, ada_b):
    depth, d, n = ada_w.shape
    nb = 1024
    cb = jnp.broadcast_to(jnp.stack([c[0], c_ctx])[:, :, None], (2, d, LANES))
    out = pl.pallas_call(
        _mod_kernel,
        grid=(depth, n // nb),
        in_specs=[
            pl.BlockSpec((2, d, LANES), lambda l, j: (0, 0, 0)),
            pl.BlockSpec((1, d, nb), lambda l, j: (l, 0, j)),
            pl.BlockSpec((1, 1, nb), lambda l, j: (l, 0, j)),
        ],
        out_specs=pl.BlockSpec((1, SUBLANES, nb), lambda l, j: (l, 0, j)),
        out_shape=jax.ShapeDtypeStruct((depth, SUBLANES, n), F32),
        compiler_params=_cparams(("arbitrary", "arbitrary")),
        name="adaln_mod",
    )(cb, ada_w, ada_b.reshape(depth, 1, n))
    mod = out[:, :2].reshape(depth, 2, N_MOD, d)
    return jnp.pad(mod, ((0, 0), (0, 0), (0, SUBLANES - N_MOD), (0, 0)))


def _inproj_kernel(x_ref, mod_ref, g_ref, w_ref, u_ref):
    h = _rms(x_ref[...], g_ref[...]) * (1.0 + mod_ref[1:2, :]) + mod_ref[0:1, :]
    u_ref[...] = jnp.dot(h.astype(BF16), w_ref[...], preferred_element_type=F32)


def _inproj(x, mod, g, w, tm):
    n, d = x.shape
    return pl.pallas_call(
        _inproj_kernel,
        grid=(n // tm,),
        in_specs=[pl.BlockSpec((tm, d), lambda i: (i, 0)), _full(mod.shape), _full(g.shape), _full(w.shape)],
        out_specs=pl.BlockSpec((tm, w.shape[1]), lambda i: (i, 0)),
        out_shape=jax.ShapeDtypeStruct((n, w.shape[1]), F32),
        compiler_params=_cparams(("arbitrary",)),
        name="inproj",
    )(x, mod, g, w)


def _s5prep_kernel(lre_ref, lim_ref, ldt_ref, bre_ref, bim_ref,
                   are_o, aim_o, bbre_o, bbim_o, pre_o, pim_o):
    lre = lre_ref[0]
    lim = lim_ref[0]
    dt = jnp.exp(ldt_ref[0])
    mag = jnp.exp(lre * dt)
    are = mag * jnp.cos(lim * dt)
    aim = mag * jnp.sin(lim * dt)
    den = lre * lre + lim * lim
    fre = ((are - 1.0) * lre + aim * lim) / den
    fim = (aim * lre - (are - 1.0) * lim) / den
    bre = bre_ref[0]
    bim = bim_ref[0]
    are_o[0] = are
    aim_o[0] = aim
    bbre_o[0] = fre * bre - fim * bim
    bbim_o[0] = fre * bim + fim * bre
    pr, pi = are, aim
    for i in range(SUB):
        pre_o[0, i:i + 1, :] = pr
        pim_o[0, i:i + 1, :] = pi
        pr, pi = pr * are - pi * aim, pr * aim + pi * are


def _s5_prepare(lam_re, lam_im, log_dt, b_re, b_im, c_re, c_im):
    nd, g, p = lam_re.shape
    gc = b_re.shape[-1]
    s = g * p
    lre = lam_re.reshape(nd, 1, s)
    lim = lam_im.reshape(nd, 1, s)
    ldt = jnp.repeat(log_dt, p, axis=-1).reshape(nd, 1, s)
    bre = jnp.transpose(b_re, (0, 3, 1, 2)).reshape(nd, gc, s)
    bim = jnp.transpose(b_im, (0, 3, 1, 2)).reshape(nd, gc, s)
    vec = pl.BlockSpec((1, 1, s), lambda d: (d, 0, 0))
    mat = pl.BlockSpec((1, gc, s), lambda d: (d, 0, 0))
    pw = pl.BlockSpec((1, SUB, s), lambda d: (d, 0, 0))
    are, aim, bbre, bbim, pre, pim = pl.pallas_call(
        _s5prep_kernel,
        grid=(nd,),
        in_specs=[vec, vec, vec, mat, mat],
        out_specs=[vec, vec, mat, mat, pw, pw],
        out_shape=[jax.ShapeDtypeStruct((nd, 1, s), F32)] * 2
        + [jax.ShapeDtypeStruct((nd, gc, s), F32)] * 2
        + [jax.ShapeDtypeStruct((nd, SUB, s), F32)] * 2,
        compiler_params=_cparams(("arbitrary",)),
        name="s5_discretize",
    )(lre, lim, ldt, bre, bim)
    a8 = jnp.broadcast_to(jnp.stack([are, aim], axis=1), (nd, 2, SUBLANES, s))
    apow = jnp.stack([pre, pim], axis=1)
    gps = LANES // gc
    nslab = g // gps
    eye = jnp.eye(gps, dtype=F32)

    def in_slabs(bb):
        t = bb.reshape(nd, gc, nslab, gps, p)
        t = jnp.einsum('dcsgp,hg->dshcgp', t, eye)
        return t.reshape(nd, nslab, LANES, gps * p)

    def out_slabs(cc):
        t = cc.reshape(nd, nslab, gps, gc, p)
        t = jnp.einsum('dsgcp,hg->dsgphc', t, eye)
        return t.reshape(nd, nslab, gps * p, LANES)

    bmat = jnp.stack([in_slabs(bbre), in_slabs(bbim)], axis=1).astype(BF16)
    cmat = jnp.stack([out_slabs(c_re), out_slabs(-c_im)], axis=1).astype(BF16)
    return a8, apow, bmat, cmat


def _scan_kernel(uc_ref, ulf_ref, ulb_ref, a8_ref, ap_ref, b_ref, c_ref,
                 yfc_ref, yfl_ref, ybc_ref, ybl_ref,
                 uin_scr, up_scr, hre, him, carry, yp_scr, yout_scr):
    k = pl.program_id(0)
    nslab = up_scr.shape[0]
    s_tot = hre.shape[1]
    cw = s_tot // nslab
    groups = CHUNK // SUBLANES

    @pl.when(k == 0)
    def _():
        carry[...] = jnp.zeros(carry.shape, F32)

    def run_dir(d, y_lat_ref, y_ctx_ref):
        rev = d == 1
        for r in range(groups):
            j, i0 = divmod(r * SUBLANES, SUB)
            for s in range(nslab):
                up_scr[s, pl.ds(i0 * SUBLANES + j, SUBLANES, stride=SUBLANES), :] = (
                    uin_scr[d, r * SUBLANES:(r + 1) * SUBLANES, s * LANES:(s + 1) * LANES])
        for s in range(nslab):
            ub = up_scr[s].astype(BF16)
            hre[:, s * cw:(s + 1) * cw] = jnp.dot(ub, b_ref[d, 0, s], preferred_element_type=F32)
            him[:, s * cw:(s + 1) * cw] = jnp.dot(ub, b_ref[d, 1, s], preferred_element_type=F32)
        for s in range(nslab):
            cols = pl.ds(s * cw, cw)
            are = a8_ref[d, 0, :, s * cw:(s + 1) * cw]
            aim = a8_ref[d, 1, :, s * cw:(s + 1) * cw]

            def step(ii, hc):
                hr, hi = hc
                i = (SUB - 1 - ii) if rev else ii
                rows = pl.ds(pl.multiple_of(i * SUBLANES, SUBLANES), SUBLANES)
                nr = are * hr - aim * hi + hre[rows, cols]
                ni = are * hi + aim * hr + him[rows, cols]
                hre[rows, cols] = nr
                him[rows, cols] = ni
                return nr, ni

            z = jnp.zeros((SUBLANES, cw), F32)
            lax.fori_loop(0, SUB, step, (z, z))
        end = 0 if rev else (SUB - 1) * SUBLANES
        ere = hre[end:end + SUBLANES, :]
        eim = him[end:end + SUBLANES, :]
        ar = ap_ref[d, 0, SUB - 1:SUB, :]
        ai = ap_ref[d, 1, SUB - 1:SUB, :]
        sr = carry[d, 0, 0:1, :]
        si = carry[d, 1, 0:1, :]
        srows, sirows = [], []
        order = range(SUBLANES - 1, -1, -1) if rev else range(SUBLANES)
        for j in order:
            srows.append(sr)
            sirows.append(si)
            sr, si = (ar * sr - ai * si + ere[j:j + 1, :], ar * si + ai * sr + eim[j:j + 1, :])
        carry[d, 0, 0:1, :] = sr
        carry[d, 1, 0:1, :] = si
        if rev:
            srows.reverse()
            sirows.reverse()
        sin_re = jnp.concatenate(srows, axis=0)
        sin_im = jnp.concatenate(sirows, axis=0)
        for s in range(nslab):
            cols = pl.ds(s * cw, cw)
            s_r = sin_re[:, s * cw:(s + 1) * cw]
            s_i = sin_im[:, s * cw:(s + 1) * cw]

            def fix(i, _):
                pidx = (SUB - 1 - i) if rev else i
                rows = pl.ds(pl.multiple_of(i * SUBLANES, SUBLANES), SUBLANES)
                pr = ap_ref[d, 0, pl.ds(pidx, 1), cols]
                pi = ap_ref[d, 1, pl.ds(pidx, 1), cols]
                hre[rows, cols] = hre[rows, cols] + (pr * s_r - pi * s_i)
                him[rows, cols] = him[rows, cols] + (pr * s_i + pi * s_r)
                return 0

            lax.fori_loop(0, SUB, fix, 0)
        for s in range(nslab):
            yp_scr[s] = (
                jnp.dot(hre[:, s * cw:(s + 1) * cw].astype(BF16), c_ref[d, 0, s], preferred_element_type=F32)
                + jnp.dot(him[:, s * cw:(s + 1) * cw].astype(BF16), c_ref[d, 1, s], preferred_element_type=F32))
        for r in range(groups):
            j, i0 = divmod(r * SUBLANES, SUB)
            for s in range(nslab):
                yout_scr[r * SUBLANES:(r + 1) * SUBLANES, s * LANES:(s + 1) * LANES] = (
                    yp_scr[s, pl.ds(i0 * SUBLANES + j, SUBLANES, stride=SUBLANES), :])
        y_lat_ref[...] = yout_scr[...]

        @pl.when(k == 0)
        def _():
            y_ctx_ref[...] = yout_scr[...]

    @pl.when(k == 0)
    def _():
        uin_scr[0] = uc_ref[...]
        uin_scr[1] = uc_ref[...]

    @pl.when(k > 0)
    def _():
        uin_scr[0] = ulf_ref[...]
        uin_scr[1] = ulb_ref[...]

    run_dir(0, yfl_ref, yfc_ref)
    run_dir(1, ybl_ref, ybc_ref)


def _s5_scan(u_ctx, u_lat, a8, apow, bmat, cmat):
    n_lat = u_lat.shape[0]
    nl = n_lat // CHUNK
    ds5 = bmat.shape[2] * LANES
    s_tot = a8.shape[-1]
    nslab = bmat.shape[2]
    fwd = lambda k: (jnp.maximum(k - 1, 0), 0)
    bwd = lambda k: (jnp.where(k == 0, nl - 1, nl - k), 0)
    zero = lambda k: (0, 0)
    blk = (CHUNK, ds5)
    return pl.pallas_call(
        _scan_kernel,
        grid=(nl + 1,),
        in_specs=[pl.BlockSpec(blk, zero), pl.BlockSpec(blk, fwd), pl.BlockSpec(blk, bwd),
                  _full(a8.shape), _full(apow.shape), _full(bmat.shape), _full(cmat.shape)],
        out_specs=[pl.BlockSpec(blk, zero), pl.BlockSpec(blk, fwd),
                   pl.BlockSpec(blk, zero), pl.BlockSpec(blk, bwd)],
        out_shape=[jax.ShapeDtypeStruct((CHUNK, ds5), F32), jax.ShapeDtypeStruct((n_lat, ds5), F32),
                   jax.ShapeDtypeStruct((CHUNK, ds5), F32), jax.ShapeDtypeStruct((n_lat, ds5), F32)],
        scratch_shapes=[
            pltpu.VMEM((2, CHUNK, ds5), F32),
            pltpu.VMEM((nslab, CHUNK, LANES), F32),
            pltpu.VMEM((CHUNK, s_tot), F32),
            pltpu.VMEM((CHUNK, s_tot), F32),
            pltpu.VMEM((2, 2, SUBLANES, s_tot), F32),
            pltpu.VMEM((nslab, CHUNK, LANES), F32),
            pltpu.VMEM((CHUNK, ds5), F32),
        ],
        compiler_params=_cparams(("arbitrary",)),
        name="s5_scan",
    )(u_ctx, u_lat, u_lat, a8, apow, bmat, cmat)


def _split3(x):
    hi = x.astype(BF16)
    r1 = x - hi.astype(F32)
    mid = r1.astype(BF16)
    lo = (r1 - mid.astype(F32)).astype(BF16)
    return hi, mid, lo


def _mixout_kernel(x_ref, u_ref, yf_ref, yb_ref, prev_ref, next_ref, mod_ref, g_ref, d_ref,
                   glu_ref, pw_ref, ps_ref, wo_ref, o_ref, *, nblk):
    i = pl.program_id(0)
    tm = x_ref.shape[0]
    ds5 = yf_ref.shape[1]
    u = u_ref[...]
    y = u[:, :ds5] * d_ref[...] + yf_ref[...] + yb_ref[...]
    zz = jnp.dot(jax.nn.gelu(y).astype(BF16), glu_ref[...], preferred_element_type=F32)
    ya = zz[:, :ds5] * jax.nn.sigmoid(zz[:, ds5:])
    ext_rows = tm + LANES
    diff = (lax.broadcasted_iota(jnp.int32, (tm, ext_rows), 1) - POOL_HALO
            - lax.broadcasted_iota(jnp.int32, (tm, ext_rows), 0))
    tcol = lax.broadcasted_iota(jnp.int32, (tm, 1), 0)
    erow = lax.broadcasted_iota(jnp.int32, (ext_rows, 1), 0) - POOL_HALO
    lo_lim = jnp.where(i > 0, -POOL_HALO, 0)
    hi_lim = jnp.where(i < nblk - 1, tm + POOL_HALO - 1, tm - 1)
    row_ok = (erow >= lo_lim) & (erow <= hi_lim)
    pg = pw_ref.shape[1]
    outs = []
    for gi, window in enumerate(POOL_WINDOWS):
        lo = window // 2
        hi = window - 1 - lo
        seg = u[:, ds5 + gi * pg:ds5 + (gi + 1) * pg]
        ext = jnp.concatenate(
            [prev_ref[:, gi * pg:(gi + 1) * pg], seg, next_ref[:, gi * pg:(gi + 1) * pg],
             jnp.zeros((ext_rows - tm - 2 * POOL_HALO, pg), F32)], axis=0)
        ext = jnp.where(row_ok, ext, 0.0)
        band = jnp.where((diff >= -lo) & (diff <= hi), 1.0, 0.0).astype(BF16)
        wsum = sum(jnp.dot(band, part, preferred_element_type=F32) for part in _split3(ext))
        count = (jnp.minimum(tcol + hi, hi_lim) - jnp.maximum(tcol - lo, lo_lim) + 1).astype(F32)
        resid = wsum / count - seg
        outs.append(jnp.dot(resid.astype(BF16), pw_ref[gi], preferred_element_type=F32))
    ybp = jnp.concatenate(outs, axis=1) * ps_ref[...]
    cat = jnp.concatenate([ya, ybp], axis=1).astype(BF16)
    mix = jnp.dot(cat, wo_ref[...], preferred_element_type=F32)
    o_ref[...] = x_ref[...] + mod_ref[2:3, :] * _rms(mix, g_ref[...])


def _mixout(x, u, yf, yb, mod, g, d, glu_w, pool_w, pool_scale, w_out, tm):
    n, dm = x.shape
    ds5 = yf.shape[1]
    nblk = n // tm
    hb = tm // POOL_HALO
    nhalo = n // POOL_HALO
    prev_map = lambda i: (jnp.maximum(i * hb - 1, 0), 1)
    next_map = lambda i: (jnp.minimum((i + 1) * hb, nhalo - 1), 1)
    row = lambda i: (i, 0)
    return pl.pallas_call(
        functools.partial(_mixout_kernel, nblk=nblk),
        grid=(nblk,),
        in_specs=[pl.BlockSpec((tm, dm), row), pl.BlockSpec((tm, dm), row),
                  pl.BlockSpec((tm, ds5), row), pl.BlockSpec((tm, ds5), row),
                  pl.BlockSpec((POOL_HALO, dm - ds5), prev_map), pl.BlockSpec((POOL_HALO, dm - ds5), next_map),
                  _full(mod.shape), _full(g.shape), _full(d.shape), _full(glu_w.shape),
                  _full(pool_w.shape), _full(pool_scale.shape), _full(w_out.shape)],
        out_specs=pl.BlockSpec((tm, dm), row),
        out_shape=jax.ShapeDtypeStruct((n, dm), F32),
        compiler_params=_cparams(("arbitrary",)),
        name="mixout",
    )(x, u, yf, yb, u, u, mod, g, d, glu_w, pool_w, pool_scale, w_out)


def _mlp_tail(x1, mod_ref, g2, g3, w1_ref, w2_ref):
    dff = w1_ref.shape[1]
    fc = 1024
    hb = (_rms(x1, g2) * (1.0 + mod_ref[4:5, :]) + mod_ref[3:4, :]).astype(BF16)
    acc = jnp.zeros(x1.shape, F32)
    for c in range(dff // fc):
        h1 = jnp.dot(hb, w1_ref[:, c * fc:(c + 1) * fc], preferred_element_type=F32)
        h1 = jnp.square(jnp.maximum(h1, 0.0)).astype(BF16)
        acc = acc + jnp.dot(h1, w2_ref[c * fc:(c + 1) * fc, :], preferred_element_type=F32)
    return x1 + mod_ref[5:6, :] * _rms(acc, g3)


def _mlp_kernel(x_ref, mod_ref, g_ref, w1_ref, w2_ref, o_ref):
    o_ref[...] = _mlp_tail(x_ref[...], mod_ref, g_ref[2:3, :], g_ref[3:4, :], w1_ref, w2_ref)


def _attnout_mlp_kernel(x_ref, a_ref, wo_ref, mod_ref, g_ref, w1_ref, w2_ref, o_ref):
    y = jnp.dot(a_ref[...], wo_ref[...], preferred_element_type=F32)
    x1 = x_ref[...] + mod_ref[2:3, :] * _rms(y, g_ref[1:2, :])
    o_ref[...] = _mlp_tail(x1, mod_ref, g_ref[2:3, :], g_ref[3:4, :], w1_ref, w2_ref)


def _mlp(x, mod, g4, w1, w2, tm, attn=None, w_out=None):
    n, d = x.shape
    row = lambda i: (i, 0)
    if attn is None:
        kern = _mlp_kernel
        args = (x, mod, g4, w1, w2)
        specs = [pl.BlockSpec((tm, d), row), _full(mod.shape), _full(g4.shape), _full(w1.shape), _full(w2.shape)]
    else:
        kern = _attnout_mlp_kernel
        args = (x, attn, w_out, mod, g4, w1, w2)
        specs = [pl.BlockSpec((tm, d), row), pl.BlockSpec((tm, attn.shape[1]), row), _full(w_out.shape),
                 _full(mod.shape), _full(g4.shape), _full(w1.shape), _full(w2.shape)]
    return pl.pallas_call(
        kern,
        grid=(n // tm,),
        in_specs=specs,
        out_specs=pl.BlockSpec((tm, d), row),
        out_shape=jax.ShapeDtypeStruct((n, d), F32),
        compiler_params=_cparams(("arbitrary",)),
        name="mlp" if attn is None else "attnout_mlp",
    )(*args)


def _swap16(x):
    lane = lax.broadcasted_iota(jnp.int32, x.shape, 1)
    return jnp.where((lane % 32) < 16, pltpu.roll(x, LANES - 16, 1), pltpu.roll(x, 16, 1))


def _head_rms(x, ones_bd, gain):
    sq = x * x
    hi = sq.astype(BF16)
    lo = (sq - hi.astype(F32)).astype(BF16)
    ssq = jnp.dot(hi, ones_bd, preferred_element_type=F32) + jnp.dot(lo, ones_bd, preferred_element_type=F32)
    return x * lax.rsqrt(ssq * (1.0 / HEAD_DIM) + EPS) * gain


def _rope(x, cos, sin):
    tiles = []
    for j in range(x.shape[1] // LANES):
        xt = x[:, j * LANES:(j + 1) * LANES]
        tiles.append(xt * cos + _swap16(xt) * sin)
    return jnp.concatenate(tiles, axis=1)


def _qkv_kernel(x_ref, mod_ref, g_ref, w_ref, qn_ref, kn_ref, cos_ref, sin_ref, oq_ref, ok_ref, dup_ref, dupt_ref,
                q_ref, kt_ref, v_ref, *, rope):
    dq = oq_ref.shape[0]
    dkv = ok_ref.shape[0]
    h = _rms(x_ref[...], g_ref[...]) * (1.0 + mod_ref[1:2, :]) + mod_ref[0:1, :]
    qkv = jnp.dot(h.astype(BF16), w_ref[...], preferred_element_type=F32)
    q = _head_rms(qkv[:, :dq], oq_ref[...], qn_ref[...])
    k = _head_rms(qkv[:, dq:dq + dkv], ok_ref[...], kn_ref[...])
    v = qkv[:, dq + dkv:]
    if rope:
        q = _rope(q, cos_ref[...], sin_ref[...])
        k = _rope(k, cos_ref[...], sin_ref[...])
    q_ref[...] = (q * (HEAD_DIM ** -0.5 * LOG2E)).astype(BF16)
    kt = lax.dot_general(dupt_ref[...], k.astype(BF16), (((1,), (1,)), ((), ())), preferred_element_type=F32)
    vv = jnp.dot(v.astype(BF16), dup_ref[...], preferred_element_type=F32)
    wide = KV_GROUP * HEAD_DIM
    for g in range(kt_ref.shape[0]):
        kt_ref[g] = kt[g * wide:(g + 1) * wide, :].astype(BF16)
        v_ref[g] = vv[:, g * wide:(g + 1) * wide].astype(BF16)


def _qkv(x, mod, g, w, qn, kn, cos, sin, consts, tm, rope):
    n, d = x.shape
    ones_q, ones_k, dup, dupt = consts
    dq = ones_q.shape[0]
    dkv = ones_k.shape[0]
    nkv = dkv // HEAD_DIM
    wide = KV_GROUP * HEAD_DIM
    row = lambda i: (i, 0)
    return pl.pallas_call(
        functools.partial(_qkv_kernel, rope=rope),
        grid=(n // tm,),
        in_specs=[pl.BlockSpec((tm, d), row), _full(mod.shape), _full(g.shape), _full(w.shape),
                  _full(qn.shape), _full(kn.shape),
                  pl.BlockSpec((tm, LANES), row), pl.BlockSpec((tm, LANES), row),
                  _full(ones_q.shape), _full(ones_k.shape), _full(dup.shape), _full(dupt.shape)],
        out_specs=[pl.BlockSpec((tm, dq), row),
                   pl.BlockSpec((nkv, wide, tm), lambda i: (0, 0, i)),
                   pl.BlockSpec((nkv, tm, wide), lambda i: (0, i, 0))],
        out_shape=[jax.ShapeDtypeStruct((n, dq), BF16),
                   jax.ShapeDtypeStruct((nkv, wide, n), BF16),
                   jax.ShapeDtypeStruct((nkv, n, wide), BF16)],
        compiler_params=_cparams(("arbitrary",)),
        name="qkv_rope" if rope else "qkv_ctx",
    )(x, mod, g, w, qn, kn, cos, sin, ones_q, ones_k, dup, dupt)


def _qkv_constants(dq, dkv):
    def ones_bd(n):
        idx = jnp.arange(n) // HEAD_DIM
        return (idx[:, None] == idx[None, :]).astype(BF16)
    wide = KV_GROUP * HEAD_DIM
    src = jnp.arange(dkv)
    dst = jnp.arange(dkv * KV_GROUP)
    dup = (src[:, None] == ((dst // wide) * HEAD_DIM + dst % HEAD_DIM)[None, :]).astype(BF16)
    return ones_bd(dq), ones_bd(dkv), dup, dup.T


def _rope_tables(n_lat):
    rows = n_lat // GRID_W
    row_ids = jnp.repeat(jnp.arange(rows), GRID_W).astype(F32)
    col_ids = jnp.tile(jnp.arange(GRID_W), rows).astype(F32)
    axis_dim = HEAD_DIM // 2
    inv_freq = ROPE_THETA ** (-jnp.arange(0, axis_dim, 2, dtype=F32) / axis_dim)
    ang_r = row_ids[:, None] * inv_freq[None, :]
    ang_c = col_ids[:, None] * inv_freq[None, :]
    cos = jnp.concatenate([jnp.cos(ang_r)] * 2 + [jnp.cos(ang_c)] * 2, axis=1)
    sin = jnp.concatenate([-jnp.sin(ang_r), jnp.sin(ang_r), -jnp.sin(ang_c), jnp.sin(ang_c)], axis=1)
    reps = LANES // HEAD_DIM
    return jnp.tile(cos, (1, reps)), jnp.tile(sin, (1, reps))


def _attn_kernel(q_ref, ktc_ref, vc_ref, ktl_ref, vl_ref, o_ref, qm_scr, m_scr, l_scr, acc_scr, *, tk):
    tq, wide = q_ref.shape
    n_lat = ktl_ref.shape[2]
    lane = lax.broadcasted_iota(jnp.int32, (1, wide), 1) // HEAD_DIM
    q = q_ref[...]
    for h in range(KV_GROUP):
        qm_scr[h] = jnp.where(lane == h, q, jnp.zeros_like(q))
    m_scr[...] = jnp.full(m_scr.shape, -jnp.inf, F32)
    l_scr[...] = jnp.zeros(l_scr.shape, F32)
    acc_scr[...] = jnp.zeros(acc_scr.shape, F32)

    def block(kt, v):
        pv = jnp.zeros((tq, wide), F32)
        alpha_w = jnp.zeros((tq, wide), F32)
        for h in range(KV_GROUP):
            s = jnp.dot(qm_scr[h], kt, preferred_element_type=F32)
            m_old = m_scr[h]
            m_new = jnp.maximum(m_old, jnp.max(s, axis=1, keepdims=True))
            alpha = jnp.exp2(m_old - m_new)
            p = jnp.exp2(s - m_new)
            l_scr[h] = alpha * l_scr[h] + jnp.sum(p, axis=1, keepdims=True)
            m_scr[h] = m_new
            vm = jnp.where(lane.reshape(1, wide) == h, v, jnp.zeros_like(v))
            pv = pv + jnp.dot(p.astype(BF16), vm, preferred_element_type=F32)
            alpha_w = jnp.where(lane == h, alpha, alpha_w)
        acc_scr[...] = acc_scr[...] * alpha_w + pv

    block(ktc_ref[0], vc_ref[0])

    def body(j, _):
        start = pl.multiple_of(j * tk, tk)
        block(ktl_ref[0, :, pl.ds(start, tk)], vl_ref[0, pl.ds(start, tk), :])
        return 0

    lax.fori_loop(0, n_lat // tk, body, 0)
    l_w = jnp.zeros((tq, wide), F32)
    for h in range(KV_GROUP):
        l_w = jnp.where(lane == h, l_scr[h], l_w)
    o_ref[...] = (acc_scr[...] / l_w).astype(o_ref.dtype)


def _attn_noshift_kernel(q_ref, ktc_ref, vc_ref, ktl_ref, vl_ref, o_ref, qm_scr, l_scr, acc_scr, *, tk):
    tq, wide = q_ref.shape
    n_lat = ktl_ref.shape[2]
    lane = lax.broadcasted_iota(jnp.int32, (1, wide), 1) // HEAD_DIM
    q = q_ref[...]
    for h in range(KV_GROUP):
        qm_scr[h] = jnp.where(lane == h, q, jnp.zeros_like(q))
    l_scr[...] = jnp.zeros(l_scr.shape, F32)
    acc_scr[...] = jnp.zeros(acc_scr.shape, F32)

    def block(kt, v):
        pv = jnp.zeros((tq, wide), F32)
        for h in range(KV_GROUP):
            p = jnp.exp2(jnp.dot(qm_scr[h], kt, preferred_element_type=F32))
            part = p[:, :LANES]
            for c in range(1, p.shape[1] // LANES):
                part = part + p[:, c * LANES:(c + 1) * LANES]
            l_scr[h] = l_scr[h] + part
            vm = jnp.where(lane == h, v, jnp.zeros_like(v))
            pv = pv + jnp.dot(p.astype(BF16), vm, preferred_element_type=F32)
        acc_scr[...] = acc_scr[...] + pv

    block(ktc_ref[0], vc_ref[0])

    def body(j, _):
        start = pl.multiple_of(j * tk, tk)
        block(ktl_ref[0, :, pl.ds(start, tk)], vl_ref[0, pl.ds(start, tk), :])
        return 0

    lax.fori_loop(0, n_lat // tk, body, 0)
    l_w = jnp.zeros((tq, wide), F32)
    for h in range(KV_GROUP):
        l_w = jnp.where(lane == h, jnp.sum(l_scr[h], axis=1, keepdims=True), l_w)
    o_ref[...] = (acc_scr[...] / l_w).astype(o_ref.dtype)


def _attention(q, kt_ctx, v_ctx, kt_lat, v_lat, score_bound, tq, tk):
    n, dq = q.shape
    nkv, wide, n_ctx = kt_ctx.shape

    def call(kern, stat_scratch, name):
        return pl.pallas_call(
            functools.partial(kern, tk=tk),
            grid=(nkv, n // tq),
            in_specs=[pl.BlockSpec((tq, wide), lambda g, i: (i, g)),
                      pl.BlockSpec((1, wide, n_ctx), lambda g, i: (g, 0, 0)),
                      pl.BlockSpec((1, n_ctx, wide), lambda g, i: (g, 0, 0)),
                      pl.BlockSpec((1, wide, n), lambda g, i: (g, 0, 0)),
                      pl.BlockSpec((1, n, wide), lambda g, i: (g, 0, 0))],
            out_specs=pl.BlockSpec((tq, wide), lambda g, i: (i, g)),
            out_shape=jax.ShapeDtypeStruct((n, dq), BF16),
            scratch_shapes=[pltpu.VMEM((KV_GROUP, tq, wide), BF16)] + stat_scratch
            + [pltpu.VMEM((tq, wide), F32)],
            compiler_params=_cparams(("arbitrary", "arbitrary")),
            name=name,
        )(q, kt_ctx, v_ctx, kt_lat, v_lat)

    stat = pltpu.VMEM((KV_GROUP, tq, 1), F32)
    return lax.cond(
        score_bound <= SAFE_EXP2_BOUND,
        lambda: call(_attn_noshift_kernel, [pltpu.VMEM((KV_GROUP, tq, LANES), F32)], "gqa_flash_noshift"),
        lambda: call(_attn_kernel, [stat, stat], "gqa_flash"))


def kernel(x, c, ctx, c_ctx, ada_w, ada_b, norm_g, mlp_w1, mlp_w2, mix_in_w, mix_out_w, s5_lambda_re, s5_lambda_im, s5_log_dt, s5_b_re, s5_b_im, s5_c_re, s5_c_im, s5_d, s5_glu_w, pool_w, pool_scale, attn_qkv_w, attn_out_w, attn_q_norm, attn_k_norm):
    assert x.shape[0] == 1 and ctx.shape[0] == 1
    xl = x[0]
    xc = ctx[0]
    n_lat, d = xl.shape
    n_ctx = xc.shape[0]
    assert n_ctx == CHUNK and n_lat % 1024 == 0 and n_lat % GRID_W == 0
    tm = 512

    mod = _modulation(c, c_ctx, ada_w, ada_b)
    w1 = mlp_w1.astype(BF16)
    w2 = mlp_w2.astype(BF16)

    g0 = norm_g[0]
    w_in = mix_in_w[0].astype(BF16)
    u_lat = _inproj(xl, mod[0, 0], g0[0:1], w_in, tm)
    u_ctx = _inproj(xc, mod[0, 1], g0[0:1], w_in, n_ctx)
    a8, apow, bmat, cmat = _s5_prepare(s5_lambda_re[0], s5_lambda_im[0], s5_log_dt[0], s5_b_re[0], s5_b_im[0],
                                       s5_c_re[0], s5_c_im[0])
    yf_ctx, yf_lat, yb_ctx, yb_lat = _s5_scan(u_ctx, u_lat, a8, apow, bmat, cmat)
    mix_args = (s5_d[0].reshape(1, -1), s5_glu_w[0].astype(BF16), pool_w[0].astype(BF16),
                pool_scale[0].reshape(1, -1), mix_out_w[0].astype(BF16))
    xl = _mixout(xl, u_lat, yf_lat, yb_lat, mod[0, 0], g0[1:2], *mix_args, tm=CHUNK)
    xc = _mixout(xc, u_ctx, yf_ctx, yb_ctx, mod[0, 1], g0[1:2], *mix_args, tm=CHUNK)
    xl = _mlp(xl, mod[0, 0], g0, w1[0], w2[0], tm)
    xc = _mlp(xc, mod[0, 1], g0, w1[0], w2[0], n_ctx)

    g1 = norm_g[1]
    w_qkv = attn_qkv_w[0].astype(BF16)
    dkv = (w_qkv.shape[1] - d) // 2
    consts = _qkv_constants(d, dkv)
    reps = d // HEAD_DIM
    qn = jnp.tile(attn_q_norm[0], reps).reshape(1, -1)
    kn = jnp.tile(attn_k_norm[0], dkv // HEAD_DIM).reshape(1, -1)
    cos, sin = _rope_tables(n_lat)
    q, kt_lat, v_lat = _qkv(xl, mod[1, 0], g1[0:1], w_qkv, qn, kn, cos, sin, consts, tm, rope=True)
    _, kt_ctx, v_ctx = _qkv(xc, mod[1, 1], g1[0:1], w_qkv, qn, kn, cos[:n_ctx], sin[:n_ctx], consts, n_ctx, rope=False)
    score_bound = (1.05 * LOG2E * HEAD_DIM ** 0.5
                   * jnp.max(jnp.abs(attn_q_norm[0])) * jnp.max(jnp.abs(attn_k_norm[0])))
    att = _attention(q, kt_ctx, v_ctx, kt_lat, v_lat, score_bound, tq=512, tk=1024)
    xl = _mlp(xl, mod[1, 0], g1, w1[1], w2[1], tm, attn=att, w_out=attn_out_w[0].astype(BF16))
    return xl[None]
```

```python
import functools
import math

import jax
import jax.numpy as jnp
from jax import lax
from jax.experimental import pallas as pl
from jax.experimental.pallas import tpu as pltpu

F32 = jnp.float32
BF16 = jnp.bfloat16

EPS = 1e-6
N_MOD = 6
S5_GROUP = 16
S5_STATE = 64
POOL_WINDOWS = (2, 4, 8, 16)
HEAD_DIM = 64
KV_GROUP = 4
GRID_W = 64
ROPE_THETA = 10000.0
LOG2E = math.log2(math.e)
SAFE_EXP2_BOUND = 60.0

LANES = 128
SUBLANES = 8
VMEM_LIMIT = 56 * 1024 * 1024

CHUNK = 256
SUB = CHUNK // SUBLANES
POOL_HALO = 8


def _cparams(sem):
    return pltpu.CompilerParams(dimension_semantics=sem, vmem_limit_bytes=VMEM_LIMIT)


def _rms(x, g):
    ms = jnp.mean(x * x, axis=-1, keepdims=True)
    return x * lax.rsqrt(ms + EPS) * g


def _full(shape):
    n = len(shape)
    return pl.BlockSpec(shape, lambda *_: (0,) * n, pipeline_mode=pl.Buffered(1))


def _mod_kernel(cb_ref, w_ref, b_ref, o_ref):
    nb = w_ref.shape[2]
    rows = []
    for r in range(2):
        s = jax.nn.silu(cb_ref[r])
        cols = []
        for j in range(nb // LANES):
            w = w_ref[0, :, j * LANES:(j + 1) * LANES]
            cols.append(jnp.sum(w * s, axis=0, keepdims=True))
        rows.append(jnp.concatenate(cols, axis=1) + b_ref[0])
    rows.append(jnp.zeros((SUBLANES - 2, nb), F32))
    o_ref[0] = jnp.concatenate(rows, axis=0)


def _modulation(c, c_ctx, ada_w, ada_b):
    depth, d, n = ada_w.shape
    nb = 1024
    cb = jnp.broadcast_to(jnp.stack([c[0], c_ctx])[:, :, None], (2, d, LANES))
    out = pl.pallas_call(
        _mod_kernel,
        grid=(depth, n // nb),
        in_specs=[
            pl.BlockSpec((2, d, LANES), lambda l, j: (0, 0, 0)),
            pl.BlockSpec((1, d, nb), lambda l, j: (l, 0, j)),
            pl.BlockSpec((1, 1, nb), lambda l, j: (l, 0, j)),
        ],
        out_specs=pl.BlockSpec((1, SUBLANES, nb), lambda l, j: (l, 0, j)),
        out_shape=jax.ShapeDtypeStruct((depth, SUBLANES, n), F32),
        compiler_params=_cparams(("arbitrary", "arbitrary")),
        name="adaln_mod",
    )(cb, ada_w, ada_b.reshape(depth, 1, n))
    mod = out[:, :2].reshape(depth, 2, N_MOD, d)
    return jnp.pad(mod, ((0, 0), (0, 0), (0, SUBLANES - N_MOD), (0, 0)))


def _inproj_kernel(x_ref, mod_ref, g_ref, w_ref, u_ref):
    h = _rms(x_ref[...], g_ref[...]) * (1.0 + mod_ref[1:2, :]) + mod_ref[0:1, :]
    u_ref[...] = jnp.dot(h.astype(BF16), w_ref[...], preferred_element_type=F32)


def _inproj(x, mod, g, w, tm):
    n, d = x.shape
    return pl.pallas_call(
        _inproj_kernel,
        grid=(n // tm,),
        in_specs=[pl.BlockSpec((tm, d), lambda i: (i, 0)), _full(mod.shape), _full(g.shape), _full(w.shape)],
        out_specs=pl.BlockSpec((tm, w.shape[1]), lambda i: (i, 0)),
        out_shape=jax.ShapeDtypeStruct((n, w.shape[1]), F32),
        compiler_params=_cparams(("arbitrary",)),
        name="inproj",
    )(x, mod, g, w)


def _s5prep_kernel(lre_ref, lim_ref, ldt_ref, bre_ref, bim_ref,
                   are_o, aim_o, bbre_o, bbim_o, pre_o, pim_o):
    lre = lre_ref[0]
    lim = lim_ref[0]
    dt = jnp.exp(ldt_ref[0])
    mag = jnp.exp(lre * dt)
    are = mag * jnp.cos(lim * dt)
    aim = mag * jnp.sin(lim * dt)
    den = lre * lre + lim * lim
    fre = ((are - 1.0) * lre + aim * lim) / den
    fim = (aim * lre - (are - 1.0) * lim) / den
    bre = bre_ref[0]
    bim = bim_ref[0]
    are_o[0] = are
    aim_o[0] = aim
    bbre_o[0] = fre * bre - fim * bim
    bbim_o[0] = fre * bim + fim * bre
    pr, pi = are, aim
    for i in range(SUB):
        pre_o[0, i:i + 1, :] = pr
        pim_o[0, i:i + 1, :] = pi
        pr, pi = pr * are - pi * aim, pr * aim + pi * are


def _s5_prepare(lam_re, lam_im, log_dt, b_re, b_im, c_re, c_im):
    nd, g, p = lam_re.shape
    gc = b_re.shape[-1]
    s = g * p
    lre = lam_re.reshape(nd, 1, s)
    lim = lam_im.reshape(nd, 1, s)
    ldt = jnp.repeat(log_dt, p, axis=-1).reshape(nd, 1, s)
    bre = jnp.transpose(b_re, (0, 3, 1, 2)).reshape(nd, gc, s)
    bim = jnp.transpose(b_im, (0, 3, 1, 2)).reshape(nd, gc, s)
    vec = pl.BlockSpec((1, 1, s), lambda d: (d, 0, 0))
    mat = pl.BlockSpec((1, gc, s), lambda d: (d, 0, 0))
    pw = pl.BlockSpec((1, SUB, s), lambda d: (d, 0, 0))
    are, aim, bbre, bbim, pre, pim = pl.pallas_call(
        _s5prep_kernel,
        grid=(nd,),
        in_specs=[vec, vec, vec, mat, mat],
        out_specs=[vec, vec, mat, mat, pw, pw],
        out_shape=[jax.ShapeDtypeStruct((nd, 1, s), F32)] * 2
        + [jax.ShapeDtypeStruct((nd, gc, s), F32)] * 2
        + [jax.ShapeDtypeStruct((nd, SUB, s), F32)] * 2,
        compiler_params=_cparams(("arbitrary",)),
        name="s5_discretize",
    )(lre, lim, ldt, bre, bim)
    a8 = jnp.broadcast_to(jnp.stack([are, aim], axis=1), (nd, 2, SUBLANES, s))
    apow = jnp.stack([pre, pim], axis=1)
    gps = LANES // gc
    nslab = g // gps
    eye = jnp.eye(gps, dtype=F32)

    def in_slabs(bb):
        t = bb.reshape(nd, gc, nslab, gps, p)
        t = jnp.einsum('dcsgp,hg->dshcgp', t, eye)
        return t.reshape(nd, nslab, LANES, gps * p)

    def out_slabs(cc):
        t = cc.reshape(nd, nslab, gps, gc, p)
        t = jnp.einsum('dsgcp,hg->dsgphc', t, eye)
        return t.reshape(nd, nslab, gps * p, LANES)

    bmat = jnp.stack([in_slabs(bbre), in_slabs(bbim)], axis=1).astype(BF16)
    cmat = jnp.stack([out_slabs(c_re), out_slabs(-c_im)], axis=1).astype(BF16)
    return a8, apow, bmat, cmat


def _scan_kernel(uc_ref, ulf_ref, ulb_ref, a8_ref, ap_ref, b_ref, c_ref,
                 yfc_ref, yfl_ref, ybc_ref, ybl_ref,
                 uin_scr, up_scr, hre, him, carry, yp_scr, yout_scr):
    k = pl.program_id(0)
    nslab = up_scr.shape[0]
    s_tot = hre.shape[1]
    cw = s_tot // nslab
    groups = CHUNK // SUBLANES

    @pl.when(k == 0)
    def _():
        carry[...] = jnp.zeros(carry.shape, F32)

    def run_dir(d, y_lat_ref, y_ctx_ref):
        rev = d == 1
        for r in range(groups):
            j, i0 = divmod(r * SUBLANES, SUB)
            for s in range(nslab):
                up_scr[s, pl.ds(i0 * SUBLANES + j, SUBLANES, stride=SUBLANES), :] = (
                    uin_scr[d, r * SUBLANES:(r + 1) * SUBLANES, s * LANES:(s + 1) * LANES])
        for s in range(nslab):
            ub = up_scr[s].astype(BF16)
            hre[:, s * cw:(s + 1) * cw] = jnp.dot(ub, b_ref[d, 0, s], preferred_element_type=F32)
            him[:, s * cw:(s + 1) * cw] = jnp.dot(ub, b_ref[d, 1, s], preferred_element_type=F32)
        for s in range(nslab):
            cols = pl.ds(s * cw, cw)
            are = a8_ref[d, 0, :, s * cw:(s + 1) * cw]
            aim = a8_ref[d, 1, :, s * cw:(s + 1) * cw]

            def step(ii, hc):
                hr, hi = hc
                i = (SUB - 1 - ii) if rev else ii
                rows = pl.ds(pl.multiple_of(i * SUBLANES, SUBLANES), SUBLANES)
                nr = are * hr - aim * hi + hre[rows, cols]
                ni = are * hi + aim * hr + him[rows, cols]
                hre[rows, cols] = nr
                him[rows, cols] = ni
                return nr, ni

            z = jnp.zeros((SUBLANES, cw), F32)
            lax.fori_loop(0, SUB, step, (z, z))
        end = 0 if rev else (SUB - 1) * SUBLANES
        ere = hre[end:end + SUBLANES, :]
        eim = him[end:end + SUBLANES, :]
        ar = ap_ref[d, 0, SUB - 1:SUB, :]
        ai = ap_ref[d, 1, SUB - 1:SUB, :]
        sr = carry[d, 0, 0:1, :]
        si = carry[d, 1, 0:1, :]
        srows, sirows = [], []
        order = range(SUBLANES - 1, -1, -1) if rev else range(SUBLANES)
        for j in order:
            srows.append(sr)
            sirows.append(si)
            sr, si = (ar * sr - ai * si + ere[j:j + 1, :], ar * si + ai * sr + eim[j:j + 1, :])
        carry[d, 0, 0:1, :] = sr
        carry[d, 1, 0:1, :] = si
        if rev:
            srows.reverse()
            sirows.reverse()
        sin_re = jnp.concatenate(srows, axis=0)
        sin_im = jnp.concatenate(sirows, axis=0)
        for s in range(nslab):
            cols = pl.ds(s * cw, cw)
            s_r = sin_re[:, s * cw:(s + 1) * cw]
            s_i = sin_im[:, s * cw:(s + 1) * cw]

            def fix(i, _):
                pidx = (SUB - 1 - i) if rev else i
                rows = pl.ds(pl.multiple_of(i * SUBLANES, SUBLANES), SUBLANES)
                pr = ap_ref[d, 0, pl.ds(pidx, 1), cols]
                pi = ap_ref[d, 1, pl.ds(pidx, 1), cols]
                hre[rows, cols] = hre[rows, cols] + (pr * s_r - pi * s_i)
                him[rows, cols] = him[rows, cols] + (pr * s_i + pi * s_r)
                return 0

            lax.fori_loop(0, SUB, fix, 0)
        for s in range(nslab):
            yp_scr[s] = (
                jnp.dot(hre[:, s * cw:(s + 1) * cw].astype(BF16), c_ref[d, 0, s], preferred_element_type=F32)
                + jnp.dot(him[:, s * cw:(s + 1) * cw].astype(BF16), c_ref[d, 1, s], preferred_element_type=F32))
        for r in range(groups):
            j, i0 = divmod(r * SUBLANES, SUB)
            for s in range(nslab):
                yout_scr[r * SUBLANES:(r + 1) * SUBLANES, s * LANES:(s + 1) * LANES] = (
                    yp_scr[s, pl.ds(i0 * SUBLANES + j, SUBLANES, stride=SUBLANES), :])
        y_lat_ref[...] = yout_scr[...]

        @pl.when(k == 0)
        def _():
            y_ctx_ref[...] = yout_scr[...]

    @pl.when(k == 0)
    def _():
        uin_scr[0] = uc_ref[...]
        uin_scr[1] = uc_ref[...]

    @pl.when(k > 0)
    def _():
        uin_scr[0] = ulf_ref[...]
        uin_scr[1] = ulb_ref[...]

    run_dir(0, yfl_ref, yfc_ref)
    run_dir(1, ybl_ref, ybc_ref)


def _s5_scan(u_ctx, u_lat, a8, apow, bmat, cmat):
    n_lat = u_lat.shape[0]
    nl = n_lat // CHUNK
    ds5 = bmat.shape[2] * LANES
    s_tot = a8.shape[-1]
    nslab = bmat.shape[2]
    fwd = lambda k: (jnp.maximum(k - 1, 0), 0)
    bwd = lambda k: (jnp.where(k == 0, nl - 1, nl - k), 0)
    zero = lambda k: (0, 0)
    blk = (CHUNK, ds5)
    return pl.pallas_call(
        _scan_kernel,
        grid=(nl + 1,),
        in_specs=[pl.BlockSpec(blk, zero), pl.BlockSpec(blk, fwd), pl.BlockSpec(blk, bwd),
                  _full(a8.shape), _full(apow.shape), _full(bmat.shape), _full(cmat.shape)],
        out_specs=[pl.BlockSpec(blk, zero), pl.BlockSpec(blk, fwd),
                   pl.BlockSpec(blk, zero), pl.BlockSpec(blk, bwd)],
        out_shape=[jax.ShapeDtypeStruct((CHUNK, ds5), F32), jax.ShapeDtypeStruct((n_lat, ds5), F32),
                   jax.ShapeDtypeStruct((CHUNK, ds5), F32), jax.ShapeDtypeStruct((n_lat, ds5), F32)],
        scratch_shapes=[
            pltpu.VMEM((2, CHUNK, ds5), F32),
            pltpu.VMEM((nslab, CHUNK, LANES), F32),
            pltpu.VMEM((CHUNK, s_tot), F32),
            pltpu.VMEM((CHUNK, s_tot), F32),
            pltpu.VMEM((2, 2, SUBLANES, s_tot), F32),
            pltpu.VMEM((nslab, CHUNK, LANES), F32),
            pltpu.VMEM((CHUNK, ds5), F32),
        ],
        compiler_params=_cparams(("arbitrary",)),
        name="s5_scan",
    )(u_ctx, u_lat, u_lat, a8, apow, bmat, cmat)


def _split3(x):
    hi = x.astype(BF16)
    r1 = x - hi.astype(F32)
    mid = r1.astype(BF16)
    lo = (r1 - mid.astype(F32)).astype(BF16)
    return hi, mid, lo


def _mixout_kernel(x_ref, u_ref, yf_ref, yb_ref, prev_ref, next_ref, mod_ref, g_ref, d_ref,
                   glu_ref, pw_ref, ps_ref, wo_ref, o_ref, *, nblk):
    i = pl.program_id(0)
    tm = x_ref.shape[0]
    ds5 = yf_ref.shape[1]
    u = u_ref[...]
    y = u[:, :ds5] * d_ref[...] + yf_ref[...] + yb_ref[...]
    zz = jnp.dot(jax.nn.gelu(y).astype(BF16), glu_ref[...], preferred_element_type=F32)
    ya = zz[:, :ds5] * jax.nn.sigmoid(zz[:, ds5:])
    ext_rows = tm + LANES
    diff = (lax.broadcasted_iota(jnp.int32, (tm, ext_rows), 1) - POOL_HALO
            - lax.broadcasted_iota(jnp.int32, (tm, ext_rows), 0))
    tcol = lax.broadcasted_iota(jnp.int32, (tm, 1), 0)
    erow = lax.broadcasted_iota(jnp.int32, (ext_rows, 1), 0) - POOL_HALO
    lo_lim = jnp.where(i > 0, -POOL_HALO, 0)
    hi_lim = jnp.where(i < nblk - 1, tm + POOL_HALO - 1, tm - 1)
    row_ok = (erow >= lo_lim) & (erow <= hi_lim)
    pg = pw_ref.shape[1]
    outs = []
    for gi, window in enumerate(POOL_WINDOWS):
        lo = window // 2
        hi = window - 1 - lo
        seg = u[:, ds5 + gi * pg:ds5 + (gi + 1) * pg]
        ext = jnp.concatenate(
            [prev_ref[:, gi * pg:(gi + 1) * pg], seg, next_ref[:, gi * pg:(gi + 1) * pg],
             jnp.zeros((ext_rows - tm - 2 * POOL_HALO, pg), F32)], axis=0)
        ext = jnp.where(row_ok, ext, 0.0)
        band = jnp.where((diff >= -lo) & (diff <= hi), 1.0, 0.0).astype(BF16)
        wsum = sum(jnp.dot(band, part, preferred_element_type=F32) for part in _split3(ext))
        count = (jnp.minimum(tcol + hi, hi_lim) - jnp.maximum(tcol - lo, lo_lim) + 1).astype(F32)
        resid = wsum / count - seg
        outs.append(jnp.dot(resid.astype(BF16), pw_ref[gi], preferred_element_type=F32))
    ybp = jnp.concatenate(outs, axis=1) * ps_ref[...]
    cat = jnp.concatenate([ya, ybp], axis=1).astype(BF16)
    mix = jnp.dot(cat, wo_ref[...], preferred_element_type=F32)
    o_ref[...] = x_ref[...] + mod_ref[2:3, :] * _rms(mix, g_ref[...])


def _mixout(x, u, yf, yb, mod, g, d, glu_w, pool_w, pool_scale, w_out, tm):
    n, dm = x.shape
    ds5 = yf.shape[1]
    nblk = n // tm
    hb = tm // POOL_HALO
    nhalo = n // POOL_HALO
    prev_map = lambda i: (jnp.maximum(i * hb - 1, 0), 1)
    next_map = lambda i: (jnp.minimum((i + 1) * hb, nhalo - 1), 1)
    row = lambda i: (i, 0)
    return pl.pallas_call(
        functools.partial(_mixout_kernel, nblk=nblk),
        grid=(nblk,),
        in_specs=[pl.BlockSpec((tm, dm), row), pl.BlockSpec((tm, dm), row),
                  pl.BlockSpec((tm, ds5), row), pl.BlockSpec((tm, ds5), row),
                  pl.BlockSpec((POOL_HALO, dm - ds5), prev_map), pl.BlockSpec((POOL_HALO, dm - ds5), next_map),
                  _full(mod.shape), _full(g.shape), _full(d.shape), _full(glu_w.shape),
                  _full(pool_w.shape), _full(pool_scale.shape), _full(w_out.shape)],
        out_specs=pl.BlockSpec((tm, dm), row),
        out_shape=jax.ShapeDtypeStruct((n, dm), F32),
        compiler_params=_cparams(("arbitrary",)),
        name="mixout",
    )(x, u, yf, yb, u, u, mod, g, d, glu_w, pool_w, pool_scale, w_out)


def _mlp_tail(x1, mod_ref, g2, g3, w1_ref, w2_ref):
    dff = w1_ref.shape[1]
    fc = 1024
    hb = (_rms(x1, g2) * (1.0 + mod_ref[4:5, :]) + mod_ref[3:4, :]).astype(BF16)
    acc = jnp.zeros(x1.shape, F32)
    for c in range(dff // fc):
        h1 = jnp.dot(hb, w1_ref[:, c * fc:(c + 1) * fc], preferred_element_type=F32)
        h1 = jnp.square(jnp.maximum(h1, 0.0)).astype(BF16)
        acc = acc + jnp.dot(h1, w2_ref[c * fc:(c + 1) * fc, :], preferred_element_type=F32)
    return x1 + mod_ref[5:6, :] * _rms(acc, g3)


def _mlp_kernel(x_ref, mod_ref, g_ref, w1_ref, w2_ref, o_ref):
    o_ref[...] = _mlp_tail(x_ref[...], mod_ref, g_ref[2:3, :], g_ref[3:4, :], w1_ref, w2_ref)


def _attnout_mlp_kernel(x_ref, a_ref, wo_ref, mod_ref, g_ref, w1_ref, w2_ref, o_ref):
    y = jnp.dot(a_ref[...], wo_ref[...], preferred_element_type=F32)
    x1 = x_ref[...] + mod_ref[2:3, :] * _rms(y, g_ref[1:2, :])
    o_ref[...] = _mlp_tail(x1, mod_ref, g_ref[2:3, :], g_ref[3:4, :], w1_ref, w2_ref)


def _mlp(x, mod, g4, w1, w2, tm, attn=None, w_out=None):
    n, d = x.shape
    row = lambda i: (i, 0)
    if attn is None:
        kern = _mlp_kernel
        args = (x, mod, g4, w1, w2)
        specs = [pl.BlockSpec((tm, d), row), _full(mod.shape), _full(g4.shape), _full(w1.shape), _full(w2.shape)]
    else:
        kern = _attnout_mlp_kernel
        args = (x, attn, w_out, mod, g4, w1, w2)
        specs = [pl.BlockSpec((tm, d), row), pl.BlockSpec((tm, attn.shape[1]), row), _full(w_out.shape),
                 _full(mod.shape), _full(g4.shape), _full(w1.shape), _full(w2.shape)]
    return pl.pallas_call(
        kern,
        grid=(n // tm,),
        in_specs=specs,
        out_specs=pl.BlockSpec((tm, d), row),
        out_shape=jax.ShapeDtypeStruct((n, d), F32),
        compiler_params=_cparams(("arbitrary",)),
        name="mlp" if attn is None else "attnout_mlp",
    )(*args)


def _swap16(x):
    lane = lax.broadcasted_iota(jnp.int32, x.shape, 1)
    return jnp.where((lane % 32) < 16, pltpu.roll(x, LANES - 16, 1), pltpu.roll(x, 16, 1))


def _head_rms(x, ones_bd, gain):
    sq = x * x
    hi = sq.astype(BF16)
    lo = (sq - hi.astype(F32)).astype(BF16)
    ssq = jnp.dot(hi, ones_bd, preferred_element_type=F32) + jnp.dot(lo, ones_bd, preferred_element_type=F32)
    return x * lax.rsqrt(ssq * (1.0 / HEAD_DIM) + EPS) * gain


def _rope(x, cos, sin):
    tiles = []
    for j in range(x.shape[1] // LANES):
        xt = x[:, j * LANES:(j + 1) * LANES]
        tiles.append(xt * cos + _swap16(xt) * sin)
    return jnp.concatenate(tiles, axis=1)


def _qkv_kernel(x_ref, mod_ref, g_ref, w_ref, qn_ref, kn_ref, cos_ref, sin_ref, oq_ref, ok_ref, dup_ref, dupt_ref,
                q_ref, kt_ref, v_ref, *, rope):
    dq = oq_ref.shape[0]
    dkv = ok_ref.shape[0]
    h = _rms(x_ref[...], g_ref[...]) * (1.0 + mod_ref[1:2, :]) + mod_ref[0:1, :]
    qkv = jnp.dot(h.astype(BF16), w_ref[...], preferred_element_type=F32)
    q = _head_rms(qkv[:, :dq], oq_ref[...], qn_ref[...])
    k = _head_rms(qkv[:, dq:dq + dkv], ok_ref[...], kn_ref[...])
    v = qkv[:, dq + dkv:]
    if rope:
        q = _rope(q, cos_ref[...], sin_ref[...])
        k = _rope(k, cos_ref[...], sin_ref[...])
    q_ref[...] = (q * (HEAD_DIM ** -0.5 * LOG2E)).astype(BF16)
    kt = lax.dot_general(dupt_ref[...], k.astype(BF16), (((1,), (1,)), ((), ())), preferred_element_type=F32)
    vv = jnp.dot(v.astype(BF16), dup_ref[...], preferred_element_type=F32)
    wide = KV_GROUP * HEAD_DIM
    for g in range(kt_ref.shape[0]):
        kt_ref[g] = kt[g * wide:(g + 1) * wide, :].astype(BF16)
        v_ref[g] = vv[:, g * wide:(g + 1) * wide].astype(BF16)


def _qkv(x, mod, g, w, qn, kn, cos, sin, consts, tm, rope):
    n, d = x.shape
    ones_q, ones_k, dup, dupt = consts
    dq = ones_q.shape[0]
    dkv = ones_k.shape[0]
    nkv = dkv // HEAD_DIM
    wide = KV_GROUP * HEAD_DIM
    row = lambda i: (i, 0)
    return pl.pallas_call(
        functools.partial(_qkv_kernel, rope=rope),
        grid=(n // tm,),
        in_specs=[pl.BlockSpec((tm, d), row), _full(mod.shape), _full(g.shape), _full(w.shape),
                  _full(qn.shape), _full(kn.shape),
                  pl.BlockSpec((tm, LANES), row), pl.BlockSpec((tm, LANES), row),
                  _full(ones_q.shape), _full(ones_k.shape), _full(dup.shape), _full(dupt.shape)],
        out_specs=[pl.BlockSpec((tm, dq), row),
                   pl.BlockSpec((nkv, wide, tm), lambda i: (0, 0, i)),
                   pl.BlockSpec((nkv, tm, wide), lambda i: (0, i, 0))],
        out_shape=[jax.ShapeDtypeStruct((n, dq), BF16),
                   jax.ShapeDtypeStruct((nkv, wide, n), BF16),
                   jax.ShapeDtypeStruct((nkv, n, wide), BF16)],
        compiler_params=_cparams(("arbitrary",)),
        name="qkv_rope" if rope else "qkv_ctx",
    )(x, mod, g, w, qn, kn, cos, sin, ones_q, ones_k, dup, dupt)


def _qkv_constants(dq, dkv):
    def ones_bd(n):
        idx = jnp.arange(n) // HEAD_DIM
        return (idx[:, None] == idx[None, :]).astype(BF16)
    wide = KV_GROUP * HEAD_DIM
    src = jnp.arange(dkv)
    dst = jnp.arange(dkv * KV_GROUP)
    dup = (src[:, None] == ((dst // wide) * HEAD_DIM + dst % HEAD_DIM)[None, :]).astype(BF16)
    return ones_bd(dq), ones_bd(dkv), dup, dup.T


def _rope_tables(n_lat):
    rows = n_lat // GRID_W
    row_ids = jnp.repeat(jnp.arange(rows), GRID_W).astype(F32)
    col_ids = jnp.tile(jnp.arange(GRID_W), rows).astype(F32)
    axis_dim = HEAD_DIM // 2
    inv_freq = ROPE_THETA ** (-jnp.arange(0, axis_dim, 2, dtype=F32) / axis_dim)
    ang_r = row_ids[:, None] * inv_freq[None, :]
    ang_c = col_ids[:, None] * inv_freq[None, :]
    cos = jnp.concatenate([jnp.cos(ang_r)] * 2 + [jnp.cos(ang_c)] * 2, axis=1)
    sin = jnp.concatenate([-jnp.sin(ang_r), jnp.sin(ang_r), -jnp.sin(ang_c), jnp.sin(ang_c)], axis=1)
    reps = LANES // HEAD_DIM
    return jnp.tile(cos, (1, reps)), jnp.tile(sin, (1, reps))


def _attn_kernel(q_ref, ktc_ref, vc_ref, ktl_ref, vl_ref, o_ref, qm_scr, m_scr, l_scr, acc_scr, *, tk):
    tq, wide = q_ref.shape
    n_lat = ktl_ref.shape[2]
    lane = lax.broadcasted_iota(jnp.int32, (1, wide), 1) // HEAD_DIM
    q = q_ref[...]
    for h in range(KV_GROUP):
        qm_scr[h] = jnp.where(lane == h, q, jnp.zeros_like(q))
    m_scr[...] = jnp.full(m_scr.shape, -jnp.inf, F32)
    l_scr[...] = jnp.zeros(l_scr.shape, F32)
    acc_scr[...] = jnp.zeros(acc_scr.shape, F32)

    def block(kt, v):
        pv = jnp.zeros((tq, wide), F32)
        alpha_w = jnp.zeros((tq, wide), F32)
        for h in range(KV_GROUP):
            s = jnp.dot(qm_scr[h], kt, preferred_element_type=F32)
            m_old = m_scr[h]
            m_new = jnp.maximum(m_old, jnp.max(s, axis=1, keepdims=True))
            alpha = jnp.exp2(m_old - m_new)
            p = jnp.exp2(s - m_new)
            l_scr[h] = alpha * l_scr[h] + jnp.sum(p, axis=1, keepdims=True)
            m_scr[h] = m_new
            vm = jnp.where(lane.reshape(1, wide) == h, v, jnp.zeros_like(v))
            pv = pv + jnp.dot(p.astype(BF16), vm, preferred_element_type=F32)
            alpha_w = jnp.where(lane == h, alpha, alpha_w)
        acc_scr[...] = acc_scr[...] * alpha_w + pv

    block(ktc_ref[0], vc_ref[0])

    def body(j, _):
        start = pl.multiple_of(j * tk, tk)
        block(ktl_ref[0, :, pl.ds(start, tk)], vl_ref[0, pl.ds(start, tk), :])
        return 0

    lax.fori_loop(0, n_lat // tk, body, 0)
    l_w = jnp.zeros((tq, wide), F32)
    for h in range(KV_GROUP):
        l_w = jnp.where(lane == h, l_scr[h], l_w)
    o_ref[...] = (acc_scr[...] / l_w).astype(o_ref.dtype)


def _attn_noshift_kernel(q_ref, ktc_ref, vc_ref, ktl_ref, vl_ref, o_ref, qm_scr, pa_scr, pb_scr, l_scr, acc_scr, *, tk):
    tq, wide = q_ref.shape
    n_lat = ktl_ref.shape[2]
    nblk = n_lat // tk
    lane = lax.broadcasted_iota(jnp.int32, (1, wide), 1) // HEAD_DIM
    q = q_ref[...]
    for h in range(KV_GROUP):
        qm_scr[h] = jnp.where(lane == h, q, jnp.zeros_like(q))
    l_scr[...] = jnp.zeros(l_scr.shape, F32)
    acc_scr[...] = jnp.zeros(acc_scr.shape, F32)

    def probs(kt):
        out = []
        for h in range(KV_GROUP):
            p = jnp.exp2(jnp.dot(qm_scr[h], kt, preferred_element_type=F32))
            part = p[:, :LANES]
            for c in range(1, p.shape[1] // LANES):
                part = part + p[:, c * LANES:(c + 1) * LANES]
            l_scr[h] = l_scr[h] + part
            out.append(p.astype(BF16))
        return out

    def weighted(ps, v):
        pv = jnp.zeros((tq, wide), F32)
        for h in range(KV_GROUP):
            vm = jnp.where(lane == h, v, jnp.zeros_like(v))
            pv = pv + jnp.dot(ps[h], vm, preferred_element_type=F32)
        acc_scr[...] = acc_scr[...] + pv

    def kblk(j):
        return ktl_ref[0, :, pl.ds(pl.multiple_of(j * tk, tk), tk)]

    def vblk(j):
        return vl_ref[0, pl.ds(pl.multiple_of(j * tk, tk), tk), :]

    def fill(scr, kt):
        for h, p in enumerate(probs(kt)):
            scr[h] = p

    def drain(scr, v):
        weighted([scr[h] for h in range(KV_GROUP)], v)

    weighted(probs(ktc_ref[0]), vc_ref[0])
    fill(pa_scr, kblk(0))

    def body(i, _):
        fill(pb_scr, kblk(2 * i + 1))
        drain(pa_scr, vblk(2 * i))
        fill(pa_scr, kblk(2 * i + 2))
        drain(pb_scr, vblk(2 * i + 1))
        return 0

    lax.fori_loop(0, nblk // 2 - 1, body, 0)
    fill(pb_scr, kblk(nblk - 1))
    drain(pa_scr, vblk(nblk - 2))
    drain(pb_scr, vblk(nblk - 1))
    l_w = jnp.zeros((tq, wide), F32)
    for h in range(KV_GROUP):
        l_w = jnp.where(lane == h, jnp.sum(l_scr[h], axis=1, keepdims=True), l_w)
    o_ref[...] = (acc_scr[...] / l_w).astype(o_ref.dtype)


def _attention(q, kt_ctx, v_ctx, kt_lat, v_lat, score_bound, tq, tk):
    n, dq = q.shape
    nkv, wide, n_ctx = kt_ctx.shape

    def call(kern, stat_scratch, name):
        return pl.pallas_call(
            functools.partial(kern, tk=tk),
            grid=(nkv, n // tq),
            in_specs=[pl.BlockSpec((tq, wide), lambda g, i: (i, g)),
                      pl.BlockSpec((1, wide, n_ctx), lambda g, i: (g, 0, 0)),
                      pl.BlockSpec((1, n_ctx, wide), lambda g, i: (g, 0, 0)),
                      pl.BlockSpec((1, wide, n), lambda g, i: (g, 0, 0), pipeline_mode=pl.Buffered(1)),
                      pl.BlockSpec((1, n, wide), lambda g, i: (g, 0, 0), pipeline_mode=pl.Buffered(1))],
            out_specs=pl.BlockSpec((tq, wide), lambda g, i: (i, g)),
            out_shape=jax.ShapeDtypeStruct((n, dq), BF16),
            scratch_shapes=[pltpu.VMEM((KV_GROUP, tq, wide), BF16)] + stat_scratch
            + [pltpu.VMEM((tq, wide), F32)],
            compiler_params=_cparams(("arbitrary", "arbitrary")),
            name=name,
        )(q, kt_ctx, v_ctx, kt_lat, v_lat)

    assert (n // tk) % 2 == 0
    stat = pltpu.VMEM((KV_GROUP, tq, 1), F32)
    probs = pltpu.VMEM((KV_GROUP, tq, tk), BF16)
    return lax.cond(
        score_bound <= SAFE_EXP2_BOUND,
        lambda: call(_attn_noshift_kernel, [probs, probs, pltpu.VMEM((KV_GROUP, tq, LANES), F32)], "gqa_flash_noshift"),
        lambda: call(_attn_kernel, [stat, stat], "gqa_flash"))


def kernel(x, c, ctx, c_ctx, ada_w, ada_b, norm_g, mlp_w1, mlp_w2, mix_in_w, mix_out_w, s5_lambda_re, s5_lambda_im, s5_log_dt, s5_b_re, s5_b_im, s5_c_re, s5_c_im, s5_d, s5_glu_w, pool_w, pool_scale, attn_qkv_w, attn_out_w, attn_q_norm, attn_k_norm):
    assert x.shape[0] == 1 and ctx.shape[0] == 1
    xl = x[0]
    xc = ctx[0]
    n_lat, d = xl.shape
    n_ctx = xc.shape[0]
    assert n_ctx == CHUNK and n_lat % 1024 == 0 and n_lat % GRID_W == 0
    tm = 512

    mod = _modulation(c, c_ctx, ada_w, ada_b)
    w1 = mlp_w1.astype(BF16)
    w2 = mlp_w2.astype(BF16)

    g0 = norm_g[0]
    w_in = mix_in_w[0].astype(BF16)
    u_lat = _inproj(xl, mod[0, 0], g0[0:1], w_in, tm)
    u_ctx = _inproj(xc, mod[0, 1], g0[0:1], w_in, n_ctx)
    a8, apow, bmat, cmat = _s5_prepare(s5_lambda_re[0], s5_lambda_im[0], s5_log_dt[0], s5_b_re[0], s5_b_im[0],
                                       s5_c_re[0], s5_c_im[0])
    yf_ctx, yf_lat, yb_ctx, yb_lat = _s5_scan(u_ctx, u_lat, a8, apow, bmat, cmat)
    mix_args = (s5_d[0].reshape(1, -1), s5_glu_w[0].astype(BF16), pool_w[0].astype(BF16),
                pool_scale[0].reshape(1, -1), mix_out_w[0].astype(BF16))
    xl = _mixout(xl, u_lat, yf_lat, yb_lat, mod[0, 0], g0[1:2], *mix_args, tm=CHUNK)
    xc = _mixout(xc, u_ctx, yf_ctx, yb_ctx, mod[0, 1], g0[1:2], *mix_args, tm=CHUNK)
    xl = _mlp(xl, mod[0, 0], g0, w1[0], w2[0], tm)
    xc = _mlp(xc, mod[0, 1], g0, w1[0], w2[0], n_ctx)

    g1 = norm_g[1]
    w_qkv = attn_qkv_w[0].astype(BF16)
    dkv = (w_qkv.shape[1] - d) // 2
    consts = _qkv_constants(d, dkv)
    reps = d // HEAD_DIM
    qn = jnp.tile(attn_q_norm[0], reps).reshape(1, -1)
    kn = jnp.tile(attn_k_norm[0], dkv // HEAD_DIM).reshape(1, -1)
    cos, sin = _rope_tables(n_lat)
    q, kt_lat, v_lat = _qkv(xl, mod[1, 0], g1[0:1], w_qkv, qn, kn, cos, sin, consts, tm, rope=True)
    _, kt_ctx, v_ctx = _qkv(xc, mod[1, 1], g1[0:1], w_qkv, qn, kn, cos[:n_ctx], sin[:n_ctx], consts, n_ctx, rope=False)
    score_bound = (1.05 * LOG2E * HEAD_DIM ** 0.5
                   * jnp.max(jnp.abs(attn_q_norm[0])) * jnp.max(jnp.abs(attn_k_norm[0])))
    att = _attention(q, kt_ctx, v_ctx, kt_lat, v_lat, score_bound, tq=512, tk=1024)
    xl = _mlp(xl, mod[1, 0], g1, w1[1], w2[1], tm, attn=att, w_out=attn_out_w[0].astype(BF16))
    return xl[None]
```

```python
import functools
import math

import jax
import jax.numpy as jnp
from jax import lax
from jax.experimental import pallas as pl
from jax.experimental.pallas import tpu as pltpu

F32 = jnp.float32
BF16 = jnp.bfloat16

EPS = 1e-6
N_MOD = 6
S5_GROUP = 16
S5_STATE = 64
POOL_WINDOWS = (2, 4, 8, 16)
HEAD_DIM = 64
KV_GROUP = 4
GRID_W = 64
ROPE_THETA = 10000.0
LOG2E = math.log2(math.e)
SAFE_EXP2_BOUND = 60.0

LANES = 128
SUBLANES = 8
VMEM_LIMIT = 56 * 1024 * 1024

CHUNK = 256
SUB = CHUNK // SUBLANES
POOL_HALO = 8
V_ROWS = HEAD_DIM + 16


def _cparams(sem):
    return pltpu.CompilerParams(dimension_semantics=sem, vmem_limit_bytes=VMEM_LIMIT)


def _rms(x, g):
    ms = jnp.mean(x * x, axis=-1, keepdims=True)
    return x * lax.rsqrt(ms + EPS) * g


def _full(shape):
    n = len(shape)
    return pl.BlockSpec(shape, lambda *_: (0,) * n, pipeline_mode=pl.Buffered(1))


def _mod_kernel(cb_ref, w_ref, b_ref, o_ref):
    nb = w_ref.shape[2]
    rows = []
    for r in range(2):
        s = jax.nn.silu(cb_ref[r])
        cols = []
        for j in range(nb // LANES):
            w = w_ref[0, :, j * LANES:(j + 1) * LANES]
            cols.append(jnp.sum(w * s, axis=0, keepdims=True))
        rows.append(jnp.concatenate(cols, axis=1) + b_ref[0])
    rows.append(jnp.zeros((SUBLANES - 2, nb), F32))
    o_ref[0] = jnp.concatenate(rows, axis=0)


def _modulation(c, c_ctx, ada_w, ada_b):
    depth, d, n = ada_w.shape
    nb = 1024
    cb = jnp.broadcast_to(jnp.stack([c[0], c_ctx])[:, :, None], (2, d, LANES))
    out = pl.pallas_call(
        _mod_kernel,
        grid=(depth, n // nb),
        in_specs=[
            pl.BlockSpec((2, d, LANES), lambda l, j: (0, 0, 0)),
            pl.BlockSpec((1, d, nb), lambda l, j: (l, 0, j)),
            pl.BlockSpec((1, 1, nb), lambda l, j: (l, 0, j)),
        ],
        out_specs=pl.BlockSpec((1, SUBLANES, nb), lambda l, j: (l, 0, j)),
        out_shape=jax.ShapeDtypeStruct((depth, SUBLANES, n), F32),
        compiler_params=_cparams(("arbitrary", "arbitrary")),
        name="adaln_mod",
    )(cb, ada_w, ada_b.reshape(depth, 1, n))
    mod = out[:, :2].reshape(depth, 2, N_MOD, d)
    return jnp.pad(mod, ((0, 0), (0, 0), (0, SUBLANES - N_MOD), (0, 0)))


def _inproj_kernel(x_ref, mod_ref, g_ref, w_ref, u_ref):
    h = _rms(x_ref[...], g_ref[...]) * (1.0 + mod_ref[1:2, :]) + mod_ref[0:1, :]
    u_ref[...] = jnp.dot(h.astype(BF16), w_ref[...], preferred_element_type=F32)


def _inproj(x, mod, g, w, tm):
    n, d = x.shape
    return pl.pallas_call(
        _inproj_kernel,
        grid=(n // tm,),
        in_specs=[pl.BlockSpec((tm, d), lambda i: (i, 0)), _full(mod.shape), _full(g.shape), _full(w.shape)],
        out_specs=pl.BlockSpec((tm, w.shape[1]), lambda i: (i, 0)),
        out_shape=jax.ShapeDtypeStruct((n, w.shape[1]), F32),
        compiler_params=_cparams(("arbitrary",)),
        name="inproj",
    )(x, mod, g, w)


def _s5prep_kernel(lre_ref, lim_ref, ldt_ref, bre_ref, bim_ref,
                   are_o, aim_o, bbre_o, bbim_o, pre_o, pim_o):
    lre = lre_ref[0]
    lim = lim_ref[0]
    dt = jnp.exp(ldt_ref[0])
    mag = jnp.exp(lre * dt)
    are = mag * jnp.cos(lim * dt)
    aim = mag * jnp.sin(lim * dt)
    den = lre * lre + lim * lim
    fre = ((are - 1.0) * lre + aim * lim) / den
    fim = (aim * lre - (are - 1.0) * lim) / den
    bre = bre_ref[0]
    bim = bim_ref[0]
    are_o[0] = are
    aim_o[0] = aim
    bbre_o[0] = fre * bre - fim * bim
    bbim_o[0] = fre * bim + fim * bre
    pr, pi = are, aim
    for i in range(SUB):
        pre_o[0, i:i + 1, :] = pr
        pim_o[0, i:i + 1, :] = pi
        pr, pi = pr * are - pi * aim, pr * aim + pi * are


def _s5_prepare(lam_re, lam_im, log_dt, b_re, b_im, c_re, c_im):
    nd, g, p = lam_re.shape
    gc = b_re.shape[-1]
    s = g * p
    lre = lam_re.reshape(nd, 1, s)
    lim = lam_im.reshape(nd, 1, s)
    ldt = jnp.repeat(log_dt, p, axis=-1).reshape(nd, 1, s)
    bre = jnp.transpose(b_re, (0, 3, 1, 2)).reshape(nd, gc, s)
    bim = jnp.transpose(b_im, (0, 3, 1, 2)).reshape(nd, gc, s)
    vec = pl.BlockSpec((1, 1, s), lambda d: (d, 0, 0))
    mat = pl.BlockSpec((1, gc, s), lambda d: (d, 0, 0))
    pw = pl.BlockSpec((1, SUB, s), lambda d: (d, 0, 0))
    are, aim, bbre, bbim, pre, pim = pl.pallas_call(
        _s5prep_kernel,
        grid=(nd,),
        in_specs=[vec, vec, vec, mat, mat],
        out_specs=[vec, vec, mat, mat, pw, pw],
        out_shape=[jax.ShapeDtypeStruct((nd, 1, s), F32)] * 2
        + [jax.ShapeDtypeStruct((nd, gc, s), F32)] * 2
        + [jax.ShapeDtypeStruct((nd, SUB, s), F32)] * 2,
        compiler_params=_cparams(("arbitrary",)),
        name="s5_discretize",
    )(lre, lim, ldt, bre, bim)
    a8 = jnp.broadcast_to(jnp.stack([are, aim], axis=1), (nd, 2, SUBLANES, s))
    apow = jnp.stack([pre, pim], axis=1)
    gps = LANES // gc
    nslab = g // gps
    eye = jnp.eye(gps, dtype=F32)

    def in_slabs(bb):
        t = bb.reshape(nd, gc, nslab, gps, p)
        t = jnp.einsum('dcsgp,hg->dshcgp', t, eye)
        return t.reshape(nd, nslab, LANES, gps * p)

    def out_slabs(cc):
        t = cc.reshape(nd, nslab, gps, gc, p)
        t = jnp.einsum('dsgcp,hg->dsgphc', t, eye)
        return t.reshape(nd, nslab, gps * p, LANES)

    bmat = jnp.stack([in_slabs(bbre), in_slabs(bbim)], axis=1).astype(BF16)
    cmat = jnp.stack([out_slabs(c_re), out_slabs(-c_im)], axis=1).astype(BF16)
    return a8, apow, bmat, cmat


def _scan_kernel(uc_ref, ulf_ref, ulb_ref, a8_ref, ap_ref, b_ref, c_ref,
                 yfc_ref, yfl_ref, ybc_ref, ybl_ref,
                 uin_scr, up_scr, hre, him, carry, yp_scr, yout_scr):
    k = pl.program_id(0)
    nslab = up_scr.shape[0]
    s_tot = hre.shape[1]
    cw = s_tot // nslab
    groups = CHUNK // SUBLANES

    @pl.when(k == 0)
    def _():
        carry[...] = jnp.zeros(carry.shape, F32)

    def run_dir(d, y_lat_ref, y_ctx_ref):
        rev = d == 1
        for r in range(groups):
            j, i0 = divmod(r * SUBLANES, SUB)
            for s in range(nslab):
                up_scr[s, pl.ds(i0 * SUBLANES + j, SUBLANES, stride=SUBLANES), :] = (
                    uin_scr[d, r * SUBLANES:(r + 1) * SUBLANES, s * LANES:(s + 1) * LANES])
        for s in range(nslab):
            ub = up_scr[s].astype(BF16)
            hre[:, s * cw:(s + 1) * cw] = jnp.dot(ub, b_ref[d, 0, s], preferred_element_type=F32)
            him[:, s * cw:(s + 1) * cw] = jnp.dot(ub, b_ref[d, 1, s], preferred_element_type=F32)
        for s in range(nslab):
            cols = pl.ds(s * cw, cw)
            are = a8_ref[d, 0, :, s * cw:(s + 1) * cw]
            aim = a8_ref[d, 1, :, s * cw:(s + 1) * cw]

            def step(ii, hc):
                hr, hi = hc
                i = (SUB - 1 - ii) if rev else ii
                rows = pl.ds(pl.multiple_of(i * SUBLANES, SUBLANES), SUBLANES)
                nr = are * hr - aim * hi + hre[rows, cols]
                ni = are * hi + aim * hr + him[rows, cols]
                hre[rows, cols] = nr
                him[rows, cols] = ni
                return nr, ni

            z = jnp.zeros((SUBLANES, cw), F32)
            lax.fori_loop(0, SUB, step, (z, z))
        end = 0 if rev else (SUB - 1) * SUBLANES
        ere = hre[end:end + SUBLANES, :]
        eim = him[end:end + SUBLANES, :]
        ar = ap_ref[d, 0, SUB - 1:SUB, :]
        ai = ap_ref[d, 1, SUB - 1:SUB, :]
        sr = carry[d, 0, 0:1, :]
        si = carry[d, 1, 0:1, :]
        srows, sirows = [], []
        order = range(SUBLANES - 1, -1, -1) if rev else range(SUBLANES)
        for j in order:
            srows.append(sr)
            sirows.append(si)
            sr, si = (ar * sr - ai * si + ere[j:j + 1, :], ar * si + ai * sr + eim[j:j + 1, :])
        carry[d, 0, 0:1, :] = sr
        carry[d, 1, 0:1, :] = si
        if rev:
            srows.reverse()
            sirows.reverse()
        sin_re = jnp.concatenate(srows, axis=0)
        sin_im = jnp.concatenate(sirows, axis=0)
        for s in range(nslab):
            cols = pl.ds(s * cw, cw)
            s_r = sin_re[:, s * cw:(s + 1) * cw]
            s_i = sin_im[:, s * cw:(s + 1) * cw]

            def fix(i, _):
                pidx = (SUB - 1 - i) if rev else i
                rows = pl.ds(pl.multiple_of(i * SUBLANES, SUBLANES), SUBLANES)
                pr = ap_ref[d, 0, pl.ds(pidx, 1), cols]
                pi = ap_ref[d, 1, pl.ds(pidx, 1), cols]
                hre[rows, cols] = hre[rows, cols] + (pr * s_r - pi * s_i)
                him[rows, cols] = him[rows, cols] + (pr * s_i + pi * s_r)
                return 0

            lax.fori_loop(0, SUB, fix, 0)
        for s in range(nslab):
            yp_scr[s] = (
                jnp.dot(hre[:, s * cw:(s + 1) * cw].astype(BF16), c_ref[d, 0, s], preferred_element_type=F32)
                + jnp.dot(him[:, s * cw:(s + 1) * cw].astype(BF16), c_ref[d, 1, s], preferred_element_type=F32))
        for r in range(groups):
            j, i0 = divmod(r * SUBLANES, SUB)
            for s in range(nslab):
                yout_scr[r * SUBLANES:(r + 1) * SUBLANES, s * LANES:(s + 1) * LANES] = (
                    yp_scr[s, pl.ds(i0 * SUBLANES + j, SUBLANES, stride=SUBLANES), :])
        y_lat_ref[...] = yout_scr[...]

        @pl.when(k == 0)
        def _():
            y_ctx_ref[...] = yout_scr[...]

    @pl.when(k == 0)
    def _():
        uin_scr[0] = uc_ref[...]
        uin_scr[1] = uc_ref[...]

    @pl.when(k > 0)
    def _():
        uin_scr[0] = ulf_ref[...]
        uin_scr[1] = ulb_ref[...]

    run_dir(0, yfl_ref, yfc_ref)
    run_dir(1, ybl_ref, ybc_ref)


def _s5_scan(u_ctx, u_lat, a8, apow, bmat, cmat):
    n_lat = u_lat.shape[0]
    nl = n_lat // CHUNK
    ds5 = bmat.shape[2] * LANES
    s_tot = a8.shape[-1]
    nslab = bmat.shape[2]
    fwd = lambda k: (jnp.maximum(k - 1, 0), 0)
    bwd = lambda k: (jnp.where(k == 0, nl - 1, nl - k), 0)
    zero = lambda k: (0, 0)
    blk = (CHUNK, ds5)
    return pl.pallas_call(
        _scan_kernel,
        grid=(nl + 1,),
        in_specs=[pl.BlockSpec(blk, zero), pl.BlockSpec(blk, fwd), pl.BlockSpec(blk, bwd),
                  _full(a8.shape), _full(apow.shape), _full(bmat.shape), _full(cmat.shape)],
        out_specs=[pl.BlockSpec(blk, zero), pl.BlockSpec(blk, fwd),
                   pl.BlockSpec(blk, zero), pl.BlockSpec(blk, bwd)],
        out_shape=[jax.ShapeDtypeStruct((CHUNK, ds5), F32), jax.ShapeDtypeStruct((n_lat, ds5), F32),
                   jax.ShapeDtypeStruct((CHUNK, ds5), F32), jax.ShapeDtypeStruct((n_lat, ds5), F32)],
        scratch_shapes=[
            pltpu.VMEM((2, CHUNK, ds5), F32),
            pltpu.VMEM((nslab, CHUNK, LANES), F32),
            pltpu.VMEM((CHUNK, s_tot), F32),
            pltpu.VMEM((CHUNK, s_tot), F32),
            pltpu.VMEM((2, 2, SUBLANES, s_tot), F32),
            pltpu.VMEM((nslab, CHUNK, LANES), F32),
            pltpu.VMEM((CHUNK, ds5), F32),
        ],
        compiler_params=_cparams(("arbitrary",)),
        name="s5_scan",
    )(u_ctx, u_lat, u_lat, a8, apow, bmat, cmat)


def _split3(x):
    hi = x.astype(BF16)
    r1 = x - hi.astype(F32)
    mid = r1.astype(BF16)
    lo = (r1 - mid.astype(F32)).astype(BF16)
    return hi, mid, lo


def _mixout_kernel(x_ref, u_ref, yf_ref, yb_ref, prev_ref, next_ref, mod_ref, g_ref, d_ref,
                   glu_ref, pw_ref, ps_ref, wo_ref, o_ref, *, nblk):
    i = pl.program_id(0)
    tm = x_ref.shape[0]
    ds5 = yf_ref.shape[1]
    u = u_ref[...]
    y = u[:, :ds5] * d_ref[...] + yf_ref[...] + yb_ref[...]
    zz = jnp.dot(jax.nn.gelu(y).astype(BF16), glu_ref[...], preferred_element_type=F32)
    ya = zz[:, :ds5] * jax.nn.sigmoid(zz[:, ds5:])
    ext_rows = tm + LANES
    diff = (lax.broadcasted_iota(jnp.int32, (tm, ext_rows), 1) - POOL_HALO
            - lax.broadcasted_iota(jnp.int32, (tm, ext_rows), 0))
    tcol = lax.broadcasted_iota(jnp.int32, (tm, 1), 0)
    erow = lax.broadcasted_iota(jnp.int32, (ext_rows, 1), 0) - POOL_HALO
    lo_lim = jnp.where(i > 0, -POOL_HALO, 0)
    hi_lim = jnp.where(i < nblk - 1, tm + POOL_HALO - 1, tm - 1)
    row_ok = (erow >= lo_lim) & (erow <= hi_lim)
    pg = pw_ref.shape[1]
    outs = []
    for gi, window in enumerate(POOL_WINDOWS):
        lo = window // 2
        hi = window - 1 - lo
        seg = u[:, ds5 + gi * pg:ds5 + (gi + 1) * pg]
        ext = jnp.concatenate(
            [prev_ref[:, gi * pg:(gi + 1) * pg], seg, next_ref[:, gi * pg:(gi + 1) * pg],
             jnp.zeros((ext_rows - tm - 2 * POOL_HALO, pg), F32)], axis=0)
        ext = jnp.where(row_ok, ext, 0.0)
        band = jnp.where((diff >= -lo) & (diff <= hi), 1.0, 0.0).astype(BF16)
        wsum = sum(jnp.dot(band, part, preferred_element_type=F32) for part in _split3(ext))
        count = (jnp.minimum(tcol + hi, hi_lim) - jnp.maximum(tcol - lo, lo_lim) + 1).astype(F32)
        resid = wsum / count - seg
        outs.append(jnp.dot(resid.astype(BF16), pw_ref[gi], preferred_element_type=F32))
    ybp = jnp.concatenate(outs, axis=1) * ps_ref[...]
    cat = jnp.concatenate([ya, ybp], axis=1).astype(BF16)
    mix = jnp.dot(cat, wo_ref[...], preferred_element_type=F32)
    o_ref[...] = x_ref[...] + mod_ref[2:3, :] * _rms(mix, g_ref[...])


def _mixout(x, u, yf, yb, mod, g, d, glu_w, pool_w, pool_scale, w_out, tm):
    n, dm = x.shape
    ds5 = yf.shape[1]
    nblk = n // tm
    hb = tm // POOL_HALO
    nhalo = n // POOL_HALO
    prev_map = lambda i: (jnp.maximum(i * hb - 1, 0), 1)
    next_map = lambda i: (jnp.minimum((i + 1) * hb, nhalo - 1), 1)
    row = lambda i: (i, 0)
    return pl.pallas_call(
        functools.partial(_mixout_kernel, nblk=nblk),
        grid=(nblk,),
        in_specs=[pl.BlockSpec((tm, dm), row), pl.BlockSpec((tm, dm), row),
                  pl.BlockSpec((tm, ds5), row), pl.BlockSpec((tm, ds5), row),
                  pl.BlockSpec((POOL_HALO, dm - ds5), prev_map), pl.BlockSpec((POOL_HALO, dm - ds5), next_map),
                  _full(mod.shape), _full(g.shape), _full(d.shape), _full(glu_w.shape),
                  _full(pool_w.shape), _full(pool_scale.shape), _full(w_out.shape)],
        out_specs=pl.BlockSpec((tm, dm), row),
        out_shape=jax.ShapeDtypeStruct((n, dm), F32),
        compiler_params=_cparams(("arbitrary",)),
        name="mixout",
    )(x, u, yf, yb, u, u, mod, g, d, glu_w, pool_w, pool_scale, w_out)


def _mlp_tail(x1, mod_ref, g2, g3, w1_ref, w2_ref):
    dff = w1_ref.shape[1]
    fc = 1024
    hb = (_rms(x1, g2) * (1.0 + mod_ref[4:5, :]) + mod_ref[3:4, :]).astype(BF16)
    acc = jnp.zeros(x1.shape, F32)
    for c in range(dff // fc):
        h1 = jnp.dot(hb, w1_ref[:, c * fc:(c + 1) * fc], preferred_element_type=F32)
        h1 = jnp.square(jnp.maximum(h1, 0.0)).astype(BF16)
        acc = acc + jnp.dot(h1, w2_ref[c * fc:(c + 1) * fc, :], preferred_element_type=F32)
    return x1 + mod_ref[5:6, :] * _rms(acc, g3)


def _mlp_kernel(x_ref, mod_ref, g_ref, w1_ref, w2_ref, o_ref):
    o_ref[...] = _mlp_tail(x_ref[...], mod_ref, g_ref[2:3, :], g_ref[3:4, :], w1_ref, w2_ref)


def _attnout_mlp_kernel(x_ref, a_ref, wo_ref, mod_ref, g_ref, w1_ref, w2_ref, o_ref):
    y = jnp.dot(a_ref[...], wo_ref[...], preferred_element_type=F32)
    x1 = x_ref[...] + mod_ref[2:3, :] * _rms(y, g_ref[1:2, :])
    o_ref[...] = _mlp_tail(x1, mod_ref, g_ref[2:3, :], g_ref[3:4, :], w1_ref, w2_ref)


def _mlp(x, mod, g4, w1, w2, tm, attn=None, w_out=None):
    n, d = x.shape
    row = lambda i: (i, 0)
    if attn is None:
        kern = _mlp_kernel
        args = (x, mod, g4, w1, w2)
        specs = [pl.BlockSpec((tm, d), row), _full(mod.shape), _full(g4.shape), _full(w1.shape), _full(w2.shape)]
    else:
        kern = _attnout_mlp_kernel
        args = (x, attn, w_out, mod, g4, w1, w2)
        specs = [pl.BlockSpec((tm, d), row), pl.BlockSpec((tm, attn.shape[1]), row), _full(w_out.shape),
                 _full(mod.shape), _full(g4.shape), _full(w1.shape), _full(w2.shape)]
    return pl.pallas_call(
        kern,
        grid=(n // tm,),
        in_specs=specs,
        out_specs=pl.BlockSpec((tm, d), row),
        out_shape=jax.ShapeDtypeStruct((n, d), F32),
        compiler_params=_cparams(("arbitrary",)),
        name="mlp" if attn is None else "attnout_mlp",
    )(*args)


def _swap16(x):
    lane = lax.broadcasted_iota(jnp.int32, x.shape, 1)
    return jnp.where((lane % 32) < 16, pltpu.roll(x, LANES - 16, 1), pltpu.roll(x, 16, 1))


def _head_rms(x, ones_bd, gain):
    sq = x * x
    hi = sq.astype(BF16)
    lo = (sq - hi.astype(F32)).astype(BF16)
    ssq = jnp.dot(hi, ones_bd, preferred_element_type=F32) + jnp.dot(lo, ones_bd, preferred_element_type=F32)
    return x * lax.rsqrt(ssq * (1.0 / HEAD_DIM) + EPS) * gain


def _rope(x, cos, sin):
    tiles = []
    for j in range(x.shape[1] // LANES):
        xt = x[:, j * LANES:(j + 1) * LANES]
        tiles.append(xt * cos + _swap16(xt) * sin)
    return jnp.concatenate(tiles, axis=1)


def _qkv_kernel(x_ref, mod_ref, g_ref, w_ref, qn_ref, kn_ref, cos_ref, sin_ref, oq_ref, ok_ref, eq_ref, ekv_ref,
                qt_ref, k_ref, vt_ref, *, rope):
    dq = oq_ref.shape[0]
    dkv = ok_ref.shape[0]
    h = _rms(x_ref[...], g_ref[...]) * (1.0 + mod_ref[1:2, :]) + mod_ref[0:1, :]
    qkv = jnp.dot(h.astype(BF16), w_ref[...], preferred_element_type=F32)
    q = _head_rms(qkv[:, :dq], oq_ref[...], qn_ref[...])
    k = _head_rms(qkv[:, dq:dq + dkv], ok_ref[...], kn_ref[...])
    v = qkv[:, dq + dkv:]
    if rope:
        q = _rope(q, cos_ref[...], sin_ref[...])
        k = _rope(k, cos_ref[...], sin_ref[...])
    qb = (q * (HEAD_DIM ** -0.5 * LOG2E)).astype(BF16)
    nt = (((1,), (1,)), ((), ()))
    qt_ref[...] = lax.dot_general(eq_ref[...], qb, nt, preferred_element_type=F32).astype(BF16)
    vt = lax.dot_general(ekv_ref[...], v.astype(BF16), nt, preferred_element_type=F32).astype(BF16)
    kb = k.astype(BF16)
    tm = x_ref.shape[0]
    pad = vt_ref.shape[1] - HEAD_DIM
    extra = jnp.where(lax.broadcasted_iota(jnp.int32, (pad, tm), 0) == 0, 1.0, 0.0).astype(BF16)
    for g in range(k_ref.shape[0]):
        k_ref[g] = kb[:, g * HEAD_DIM:(g + 1) * HEAD_DIM]
        vt_ref[g, :HEAD_DIM, :] = vt[g * HEAD_DIM:(g + 1) * HEAD_DIM, :]
        vt_ref[g, HEAD_DIM:, :] = extra


def _qkv(x, mod, g, w, qn, kn, cos, sin, consts, tm, rope):
    n, d = x.shape
    ones_q, ones_k, eye_q, eye_kv = consts
    dq = ones_q.shape[0]
    dkv = ones_k.shape[0]
    nkv = dkv // HEAD_DIM
    row = lambda i: (i, 0)
    return pl.pallas_call(
        functools.partial(_qkv_kernel, rope=rope),
        grid=(n // tm,),
        in_specs=[pl.BlockSpec((tm, d), row), _full(mod.shape), _full(g.shape), _full(w.shape),
                  _full(qn.shape), _full(kn.shape),
                  pl.BlockSpec((tm, LANES), row), pl.BlockSpec((tm, LANES), row),
                  _full(ones_q.shape), _full(ones_k.shape), _full(eye_q.shape), _full(eye_kv.shape)],
        out_specs=[pl.BlockSpec((dq, tm), lambda i: (0, i)),
                   pl.BlockSpec((nkv, tm, HEAD_DIM), lambda i: (0, i, 0)),
                   pl.BlockSpec((nkv, V_ROWS, tm), lambda i: (0, 0, i))],
        out_shape=[jax.ShapeDtypeStruct((dq, n), BF16),
                   jax.ShapeDtypeStruct((nkv, n, HEAD_DIM), BF16),
                   jax.ShapeDtypeStruct((nkv, V_ROWS, n), BF16)],
        compiler_params=_cparams(("arbitrary",)),
        name="qkv_rope" if rope else "qkv_ctx",
    )(x, mod, g, w, qn, kn, cos, sin, ones_q, ones_k, eye_q, eye_kv)


def _qkv_constants(dq, dkv):
    def ones_bd(n):
        idx = jnp.arange(n) // HEAD_DIM
        return (idx[:, None] == idx[None, :]).astype(BF16)
    return ones_bd(dq), ones_bd(dkv), jnp.eye(dq, dtype=BF16), jnp.eye(dkv, dtype=BF16)


def _rope_tables(n_lat):
    rows = n_lat // GRID_W
    row_ids = jnp.repeat(jnp.arange(rows), GRID_W).astype(F32)
    col_ids = jnp.tile(jnp.arange(GRID_W), rows).astype(F32)
    axis_dim = HEAD_DIM // 2
    inv_freq = ROPE_THETA ** (-jnp.arange(0, axis_dim, 2, dtype=F32) / axis_dim)
    ang_r = row_ids[:, None] * inv_freq[None, :]
    ang_c = col_ids[:, None] * inv_freq[None, :]
    cos = jnp.concatenate([jnp.cos(ang_r)] * 2 + [jnp.cos(ang_c)] * 2, axis=1)
    sin = jnp.concatenate([-jnp.sin(ang_r), jnp.sin(ang_r), -jnp.sin(ang_c), jnp.sin(ang_c)], axis=1)
    reps = LANES // HEAD_DIM
    return jnp.tile(cos, (1, reps)), jnp.tile(sin, (1, reps))


def _attn_kernel(qt_ref, kc_ref, vtc_ref, kl_ref, vtl_ref, o_ref, pa_scr, pb_scr, al_scr, m_scr, acc_scr,
                 *, tk, shift):
    tq = qt_ref.shape[1]
    n_lat = kl_ref.shape[1]
    nblk = n_lat // tk
    acc_scr[...] = jnp.zeros(acc_scr.shape, F32)
    if shift:
        m_scr[...] = jnp.full(m_scr.shape, -jnp.inf, F32)

    def probs(k):
        out = []
        for h in range(KV_GROUP):
            s = jnp.dot(k, qt_ref[h * HEAD_DIM:(h + 1) * HEAD_DIM, :], preferred_element_type=F32)
            if shift:
                m_old = m_scr[h]
                m_new = jnp.maximum(m_old, jnp.max(s, axis=0, keepdims=True))
                m_scr[h] = m_new
                out.append((jnp.exp2(s - m_new).astype(BF16), jnp.exp2(m_old - m_new)))
            else:
                out.append((jnp.exp2(s).astype(BF16), None))
        return out

    def weighted(ps, vt):
        for h, (p, alpha) in enumerate(ps):
            pv = jnp.dot(vt, p, preferred_element_type=F32)
            acc_scr[h] = (acc_scr[h] * alpha + pv) if shift else (acc_scr[h] + pv)

    def kblk(j):
        return kl_ref[0, pl.ds(pl.multiple_of(j * tk, tk), tk), :]

    def vblk(j):
        return vtl_ref[0, :, pl.ds(pl.multiple_of(j * tk, tk), tk)]

    def fill(slot, k):
        scr = (pa_scr, pb_scr)[slot]
        for h, (p, alpha) in enumerate(probs(k)):
            scr[h] = p
            if shift:
                al_scr[slot, h] = alpha

    def drain(slot, vt):
        scr = (pa_scr, pb_scr)[slot]
        weighted([(scr[h], al_scr[slot, h] if shift else None) for h in range(KV_GROUP)], vt)

    weighted(probs(kc_ref[0]), vtc_ref[0])
    fill(0, kblk(0))

    def body(i, _):
        fill(1, kblk(2 * i + 1))
        drain(0, vblk(2 * i))
        fill(0, kblk(2 * i + 2))
        drain(1, vblk(2 * i + 1))
        return 0

    lax.fori_loop(0, nblk // 2 - 1, body, 0)
    fill(1, kblk(nblk - 1))
    drain(0, vblk(nblk - 2))
    drain(1, vblk(nblk - 1))
    outs = [acc_scr[h, :HEAD_DIM, :] / acc_scr[h, HEAD_DIM:HEAD_DIM + 1, :] for h in range(KV_GROUP)]
    o_ref[...] = jnp.transpose(jnp.concatenate(outs, axis=0)).astype(o_ref.dtype)


def _attention(qt, k_ctx, vt_ctx, k_lat, vt_lat, score_bound, tq, tk):
    dq, n = qt.shape
    nkv, n_ctx, _ = k_ctx.shape
    wide = KV_GROUP * HEAD_DIM
    assert (n // tk) % 2 == 0

    def call(shift, name):
        once = dict(pipeline_mode=pl.Buffered(1))
        return pl.pallas_call(
            functools.partial(_attn_kernel, tk=tk, shift=shift),
            grid=(nkv, n // tq),
            in_specs=[pl.BlockSpec((wide, tq), lambda g, i: (g, i)),
                      pl.BlockSpec((1, n_ctx, HEAD_DIM), lambda g, i: (g, 0, 0)),
                      pl.BlockSpec((1, V_ROWS, n_ctx), lambda g, i: (g, 0, 0)),
                      pl.BlockSpec((1, n, HEAD_DIM), lambda g, i: (g, 0, 0), **once),
                      pl.BlockSpec((1, V_ROWS, n), lambda g, i: (g, 0, 0), **once)],
            out_specs=pl.BlockSpec((tq, wide), lambda g, i: (i, g)),
            out_shape=jax.ShapeDtypeStruct((n, dq), BF16),
            scratch_shapes=[pltpu.VMEM((KV_GROUP, tk, tq), BF16),
                            pltpu.VMEM((KV_GROUP, tk, tq), BF16),
                            pltpu.VMEM((2, KV_GROUP, 1, tq), F32),
                            pltpu.VMEM((KV_GROUP, 1, tq), F32),
                            pltpu.VMEM((KV_GROUP, V_ROWS, tq), F32)],
            compiler_params=_cparams(("arbitrary", "arbitrary")),
            name=name,
        )(qt, k_ctx, vt_ctx, k_lat, vt_lat)

    return lax.cond(score_bound <= SAFE_EXP2_BOUND,
                    lambda: call(False, "gqa_flash_noshift"), lambda: call(True, "gqa_flash"))


def kernel(x, c, ctx, c_ctx, ada_w, ada_b, norm_g, mlp_w1, mlp_w2, mix_in_w, mix_out_w, s5_lambda_re, s5_lambda_im, s5_log_dt, s5_b_re, s5_b_im, s5_c_re, s5_c_im, s5_d, s5_glu_w, pool_w, pool_scale, attn_qkv_w, attn_out_w, attn_q_norm, attn_k_norm):
    assert x.shape[0] == 1 and ctx.shape[0] == 1
    xl = x[0]
    xc = ctx[0]
    n_lat, d = xl.shape
    n_ctx = xc.shape[0]
    assert n_ctx == CHUNK and n_lat % 1024 == 0 and n_lat % GRID_W == 0
    tm = 512

    mod = _modulation(c, c_ctx, ada_w, ada_b)
    w1 = mlp_w1.astype(BF16)
    w2 = mlp_w2.astype(BF16)

    g0 = norm_g[0]
    w_in = mix_in_w[0].astype(BF16)
    u_lat = _inproj(xl, mod[0, 0], g0[0:1], w_in, tm)
    u_ctx = _inproj(xc, mod[0, 1], g0[0:1], w_in, n_ctx)
    a8, apow, bmat, cmat = _s5_prepare(s5_lambda_re[0], s5_lambda_im[0], s5_log_dt[0], s5_b_re[0], s5_b_im[0],
                                       s5_c_re[0], s5_c_im[0])
    yf_ctx, yf_lat, yb_ctx, yb_lat = _s5_scan(u_ctx, u_lat, a8, apow, bmat, cmat)
    mix_args = (s5_d[0].reshape(1, -1), s5_glu_w[0].astype(BF16), pool_w[0].astype(BF16),
                pool_scale[0].reshape(1, -1), mix_out_w[0].astype(BF16))
    xl = _mixout(xl, u_lat, yf_lat, yb_lat, mod[0, 0], g0[1:2], *mix_args, tm=CHUNK)
    xc = _mixout(xc, u_ctx, yf_ctx, yb_ctx, mod[0, 1], g0[1:2], *mix_args, tm=CHUNK)
    xl = _mlp(xl, mod[0, 0], g0, w1[0], w2[0], tm)
    xc = _mlp(xc, mod[0, 1], g0, w1[0], w2[0], n_ctx)

    g1 = norm_g[1]
    w_qkv = attn_qkv_w[0].astype(BF16)
    dkv = (w_qkv.shape[1] - d) // 2
    consts = _qkv_constants(d, dkv)
    reps = d // HEAD_DIM
    qn = jnp.tile(attn_q_norm[0], reps).reshape(1, -1)
    kn = jnp.tile(attn_k_norm[0], dkv // HEAD_DIM).reshape(1, -1)
    cos, sin = _rope_tables(n_lat)
    qt, k_lat, vt_lat = _qkv(xl, mod[1, 0], g1[0:1], w_qkv, qn, kn, cos, sin, consts, tm, rope=True)
    _, k_ctx, vt_ctx = _qkv(xc, mod[1, 1], g1[0:1], w_qkv, qn, kn, cos[:n_ctx], sin[:n_ctx], consts, n_ctx, rope=False)
    score_bound = (1.05 * LOG2E * HEAD_DIM ** 0.5
                   * jnp.max(jnp.abs(attn_q_norm[0])) * jnp.max(jnp.abs(attn_k_norm[0])))
    att = _attention(qt, k_ctx, vt_ctx, k_lat, vt_lat, score_bound, tq=512, tk=1024)
    xl = _mlp(xl, mod[1, 0], g1, w1[1], w2[1], tm, attn=att, w_out=attn_out_w[0].astype(BF16))
    return xl[None]
```

```python
import functools
import math

import jax
import jax.numpy as jnp
from jax import lax
from jax.experimental import pallas as pl
from jax.experimental.pallas import tpu as pltpu

F32 = jnp.float32
BF16 = jnp.bfloat16

EPS = 1e-6
N_MOD = 6
S5_GROUP = 16
S5_STATE = 64
POOL_WINDOWS = (2, 4, 8, 16)
HEAD_DIM = 64
KV_GROUP = 4
GRID_W = 64
ROPE_THETA = 10000.0
LOG2E = math.log2(math.e)
SAFE_EXP2_BOUND = 60.0

LANES = 128
SUBLANES = 8
VMEM_LIMIT = 56 * 1024 * 1024

CHUNK = 256
SUB = CHUNK // SUBLANES
POOL_HALO = 8
V_ROWS = HEAD_DIM + 16


def _cparams(sem):
    return pltpu.CompilerParams(dimension_semantics=sem, vmem_limit_bytes=VMEM_LIMIT)


def _rms(x, g):
    ms = jnp.mean(x * x, axis=-1, keepdims=True)
    return x * lax.rsqrt(ms + EPS) * g


def _full(shape):
    n = len(shape)
    return pl.BlockSpec(shape, lambda *_: (0,) * n, pipeline_mode=pl.Buffered(1))


def _mod_kernel(cb_ref, w_ref, b_ref, o_ref):
    nb = w_ref.shape[2]
    rows = []
    for r in range(2):
        s = jax.nn.silu(cb_ref[r])
        cols = []
        for j in range(nb // LANES):
            w = w_ref[0, :, j * LANES:(j + 1) * LANES]
            cols.append(jnp.sum(w * s, axis=0, keepdims=True))
        rows.append(jnp.concatenate(cols, axis=1) + b_ref[0])
    rows.append(jnp.zeros((SUBLANES - 2, nb), F32))
    o_ref[0] = jnp.concatenate(rows, axis=0)


def _modulation(c, c_ctx, ada_w, ada_b):
    depth, d, n = ada_w.shape
    nb = 1024
    cb = jnp.broadcast_to(jnp.stack([c[0], c_ctx])[:, :, None], (2, d, LANES))
    out = pl.pallas_call(
        _mod_kernel,
        grid=(depth, n // nb),
        in_specs=[
            pl.BlockSpec((2, d, LANES), lambda l, j: (0, 0, 0)),
            pl.BlockSpec((1, d, nb), lambda l, j: (l, 0, j)),
            pl.BlockSpec((1, 1, nb), lambda l, j: (l, 0, j)),
        ],
        out_specs=pl.BlockSpec((1, SUBLANES, nb), lambda l, j: (l, 0, j)),
        out_shape=jax.ShapeDtypeStruct((depth, SUBLANES, n), F32),
        compiler_params=_cparams(("arbitrary", "arbitrary")),
        name="adaln_mod",
    )(cb, ada_w, ada_b.reshape(depth, 1, n))
    mod = out[:, :2].reshape(depth, 2, N_MOD, d)
    return jnp.pad(mod, ((0, 0), (0, 0), (0, SUBLANES - N_MOD), (0, 0)))


def _inproj_kernel(x_ref, mod_ref, g_ref, w_ref, u_ref):
    h = _rms(x_ref[...], g_ref[...]) * (1.0 + mod_ref[1:2, :]) + mod_ref[0:1, :]
    u_ref[...] = jnp.dot(h.astype(BF16), w_ref[...], preferred_element_type=F32)


def _inproj(x, mod, g, w, tm):
    n, d = x.shape
    return pl.pallas_call(
        _inproj_kernel,
        grid=(n // tm,),
        in_specs=[pl.BlockSpec((tm, d), lambda i: (i, 0)), _full(mod.shape), _full(g.shape), _full(w.shape)],
        out_specs=pl.BlockSpec((tm, w.shape[1]), lambda i: (i, 0)),
        out_shape=jax.ShapeDtypeStruct((n, w.shape[1]), F32),
        compiler_params=_cparams(("arbitrary",)),
        name="inproj",
    )(x, mod, g, w)


def _s5prep_kernel(lre_ref, lim_ref, ldt_ref, bre_ref, bim_ref,
                   are_o, aim_o, bbre_o, bbim_o, pre_o, pim_o):
    lre = lre_ref[0]
    lim = lim_ref[0]
    dt = jnp.exp(ldt_ref[0])
    mag = jnp.exp(lre * dt)
    are = mag * jnp.cos(lim * dt)
    aim = mag * jnp.sin(lim * dt)
    den = lre * lre + lim * lim
    fre = ((are - 1.0) * lre + aim * lim) / den
    fim = (aim * lre - (are - 1.0) * lim) / den
    bre = bre_ref[0]
    bim = bim_ref[0]
    are_o[0] = are
    aim_o[0] = aim
    bbre_o[0] = fre * bre - fim * bim
    bbim_o[0] = fre * bim + fim * bre
    pr, pi = are, aim
    for i in range(SUB):
        pre_o[0, i:i + 1, :] = pr
        pim_o[0, i:i + 1, :] = pi
        pr, pi = pr * are - pi * aim, pr * aim + pi * are


def _s5_prepare(lam_re, lam_im, log_dt, b_re, b_im, c_re, c_im):
    nd, g, p = lam_re.shape
    gc = b_re.shape[-1]
    s = g * p
    lre = lam_re.reshape(nd, 1, s)
    lim = lam_im.reshape(nd, 1, s)
    ldt = jnp.repeat(log_dt, p, axis=-1).reshape(nd, 1, s)
    bre = jnp.transpose(b_re, (0, 3, 1, 2)).reshape(nd, gc, s)
    bim = jnp.transpose(b_im, (0, 3, 1, 2)).reshape(nd, gc, s)
    vec = pl.BlockSpec((1, 1, s), lambda d: (d, 0, 0))
    mat = pl.BlockSpec((1, gc, s), lambda d: (d, 0, 0))
    pw = pl.BlockSpec((1, SUB, s), lambda d: (d, 0, 0))
    are, aim, bbre, bbim, pre, pim = pl.pallas_call(
        _s5prep_kernel,
        grid=(nd,),
        in_specs=[vec, vec, vec, mat, mat],
        out_specs=[vec, vec, mat, mat, pw, pw],
        out_shape=[jax.ShapeDtypeStruct((nd, 1, s), F32)] * 2
        + [jax.ShapeDtypeStruct((nd, gc, s), F32)] * 2
        + [jax.ShapeDtypeStruct((nd, SUB, s), F32)] * 2,
        compiler_params=_cparams(("arbitrary",)),
        name="s5_discretize",
    )(lre, lim, ldt, bre, bim)
    a8 = jnp.broadcast_to(jnp.stack([are, aim], axis=1), (nd, 2, SUBLANES, s))
    apow = jnp.stack([pre, pim], axis=1)
    gps = LANES // gc
    nslab = g // gps
    eye = jnp.eye(gps, dtype=F32)

    def in_slabs(bb):
        t = bb.reshape(nd, gc, nslab, gps, p)
        t = jnp.einsum('dcsgp,hg->dshcgp', t, eye)
        return t.reshape(nd, nslab, LANES, gps * p)

    def out_slabs(cc):
        t = cc.reshape(nd, nslab, gps, gc, p)
        t = jnp.einsum('dsgcp,hg->dsgphc', t, eye)
        return t.reshape(nd, nslab, gps * p, LANES)

    bmat = jnp.stack([in_slabs(bbre), in_slabs(bbim)], axis=1).astype(BF16)
    cmat = jnp.stack([out_slabs(c_re), out_slabs(-c_im)], axis=1).astype(BF16)
    return a8, apow, bmat, cmat


def _scan_kernel(uc_ref, ulf_ref, ulb_ref, a8_ref, ap_ref, b_ref, c_ref,
                 yfc_ref, yfl_ref, ybc_ref, ybl_ref,
                 uin_scr, up_scr, bu_scr, hb_scr, carry, yp_scr, yout_scr):
    k = pl.program_id(0)
    nslab = up_scr.shape[1]
    cw = bu_scr.shape[3] // nslab
    groups = CHUNK // SUBLANES
    dirs = (0, 1)

    @pl.when(k == 0)
    def _():
        carry[...] = jnp.zeros(carry.shape, F32)
        uin_scr[0] = uc_ref[...]
        uin_scr[1] = uc_ref[...]

    @pl.when(k > 0)
    def _():
        uin_scr[0] = ulf_ref[...]
        uin_scr[1] = ulb_ref[...]

    for d in dirs:
        for r in range(groups):
            j, i0 = divmod(r * SUBLANES, SUB)
            for s in range(nslab):
                up_scr[d, s, pl.ds(i0 * SUBLANES + j, SUBLANES, stride=SUBLANES), :] = (
                    uin_scr[d, r * SUBLANES:(r + 1) * SUBLANES, s * LANES:(s + 1) * LANES])

    def recurrence(s, init, emit):
        cols = slice(s * cw, (s + 1) * cw)
        coef = [(a8_ref[d, 0, :, cols], a8_ref[d, 1, :, cols]) for d in dirs]
        h = list(init)
        for ii in range(SUB):
            for d in dirs:
                i = (SUB - 1 - ii) if d == 1 else ii
                rows = slice(i * SUBLANES, (i + 1) * SUBLANES)
                ar, ai = coef[d]
                hr, hi = h[d]
                h[d] = (ar * hr - ai * hi + bu_scr[d, 0, rows, cols], ar * hi + ai * hr + bu_scr[d, 1, rows, cols])
                emit(d, i, h[d])
        return h

    def project(s):
        cols = slice(s * cw, (s + 1) * cw)
        for d in dirs:
            ub = up_scr[d, s].astype(BF16)
            bu_scr[d, 0, :, cols] = jnp.dot(ub, b_ref[d, 0, s], preferred_element_type=F32)
            bu_scr[d, 1, :, cols] = jnp.dot(ub, b_ref[d, 1, s], preferred_element_type=F32)

    def readout(s):
        cols = slice(s * cw, (s + 1) * cw)
        for d in dirs:
            yp_scr[d, s] = (
                jnp.dot(hb_scr[d, 0, :, cols], c_ref[d, 0, s], preferred_element_type=F32)
                + jnp.dot(hb_scr[d, 1, :, cols], c_ref[d, 1, s], preferred_element_type=F32))

    def sweeps(s):
        cols = slice(s * cw, (s + 1) * cw)
        z = jnp.zeros((SUBLANES, cw), F32)
        ends = recurrence(s, [(z, z), (z, z)], lambda d, i, h: None)
        init = []
        for d in dirs:
            ere, eim = ends[d]
            ar = ap_ref[d, 0, SUB - 1:SUB, cols]
            ai = ap_ref[d, 1, SUB - 1:SUB, cols]
            sr = carry[d, 0, 0:1, cols]
            si = carry[d, 1, 0:1, cols]
            srows, sirows = [], []
            for j in (range(SUBLANES - 1, -1, -1) if d == 1 else range(SUBLANES)):
                srows.append(sr)
                sirows.append(si)
                sr, si = (ar * sr - ai * si + ere[j:j + 1, :], ar * si + ai * sr + eim[j:j + 1, :])
            carry[d, 0, 0:1, cols] = sr
            carry[d, 1, 0:1, cols] = si
            if d == 1:
                srows.reverse()
                sirows.reverse()
            init.append((jnp.concatenate(srows, axis=0), jnp.concatenate(sirows, axis=0)))
        pend = {}

        def emit(d, i, h):
            if i // 2 not in pend.setdefault(d, {}):
                pend[d][i // 2] = (i, h)
                return
            i0, h0 = pend[d].pop(i // 2)
            lo, hi = (h0, h) if i0 < i else (h, h0)
            rows = slice((i // 2) * 2 * SUBLANES, (i // 2 + 1) * 2 * SUBLANES)
            hb_scr[d, 0, rows, cols] = jnp.concatenate([lo[0], hi[0]], axis=0).astype(BF16)
            hb_scr[d, 1, rows, cols] = jnp.concatenate([lo[1], hi[1]], axis=0).astype(BF16)

        recurrence(s, init, emit)

    project(0)
    for s in range(nslab):
        if s + 1 < nslab:
            project(s + 1)
        if s > 0:
            readout(s - 1)
        sweeps(s)
    readout(nslab - 1)

    for d, y_lat_ref in zip(dirs, (yfl_ref, ybl_ref)):
        for r in range(groups):
            j, i0 = divmod(r * SUBLANES, SUB)
            for s in range(nslab):
                yout_scr[d, r * SUBLANES:(r + 1) * SUBLANES, s * LANES:(s + 1) * LANES] = (
                    yp_scr[d, s, pl.ds(i0 * SUBLANES + j, SUBLANES, stride=SUBLANES), :])
        y_lat_ref[...] = yout_scr[d]

    @pl.when(k == 0)
    def _():
        yfc_ref[...] = yout_scr[0]
        ybc_ref[...] = yout_scr[1]


def _s5_scan(u_ctx, u_lat, a8, apow, bmat, cmat):
    n_lat = u_lat.shape[0]
    nl = n_lat // CHUNK
    ds5 = bmat.shape[2] * LANES
    s_tot = a8.shape[-1]
    nslab = bmat.shape[2]
    fwd = lambda k: (jnp.maximum(k - 1, 0), 0)
    bwd = lambda k: (jnp.where(k == 0, nl - 1, nl - k), 0)
    zero = lambda k: (0, 0)
    blk = (CHUNK, ds5)
    return pl.pallas_call(
        _scan_kernel,
        grid=(nl + 1,),
        in_specs=[pl.BlockSpec(blk, zero), pl.BlockSpec(blk, fwd), pl.BlockSpec(blk, bwd),
                  _full(a8.shape), _full(apow.shape), _full(bmat.shape), _full(cmat.shape)],
        out_specs=[pl.BlockSpec(blk, zero), pl.BlockSpec(blk, fwd),
                   pl.BlockSpec(blk, zero), pl.BlockSpec(blk, bwd)],
        out_shape=[jax.ShapeDtypeStruct((CHUNK, ds5), F32), jax.ShapeDtypeStruct((n_lat, ds5), F32),
                   jax.ShapeDtypeStruct((CHUNK, ds5), F32), jax.ShapeDtypeStruct((n_lat, ds5), F32)],
        scratch_shapes=[
            pltpu.VMEM((2, CHUNK, ds5), F32),
            pltpu.VMEM((2, nslab, CHUNK, LANES), F32),
            pltpu.VMEM((2, 2, CHUNK, s_tot), F32),
            pltpu.VMEM((2, 2, CHUNK, s_tot), BF16),
            pltpu.VMEM((2, 2, SUBLANES, s_tot), F32),
            pltpu.VMEM((2, nslab, CHUNK, LANES), F32),
            pltpu.VMEM((2, CHUNK, ds5), F32),
        ],
        compiler_params=_cparams(("arbitrary",)),
        name="s5_scan",
    )(u_ctx, u_lat, u_lat, a8, apow, bmat, cmat)


def _split3(x):
    hi = x.astype(BF16)
    r1 = x - hi.astype(F32)
    mid = r1.astype(BF16)
    lo = (r1 - mid.astype(F32)).astype(BF16)
    return hi, mid, lo


def _mixout_kernel(x_ref, u_ref, yf_ref, yb_ref, prev_ref, next_ref, mod_ref, g_ref, d_ref,
                   glu_ref, pw_ref, ps_ref, wo_ref, o_ref, *, nblk):
    i = pl.program_id(0)
    tm = x_ref.shape[0]
    ds5 = yf_ref.shape[1]
    u = u_ref[...]
    y = u[:, :ds5] * d_ref[...] + yf_ref[...] + yb_ref[...]
    zz = jnp.dot(jax.nn.gelu(y).astype(BF16), glu_ref[...], preferred_element_type=F32)
    ya = zz[:, :ds5] * jax.nn.sigmoid(zz[:, ds5:])
    ext_rows = tm + LANES
    diff = (lax.broadcasted_iota(jnp.int32, (tm, ext_rows), 1) - POOL_HALO
            - lax.broadcasted_iota(jnp.int32, (tm, ext_rows), 0))
    tcol = lax.broadcasted_iota(jnp.int32, (tm, 1), 0)
    erow = lax.broadcasted_iota(jnp.int32, (ext_rows, 1), 0) - POOL_HALO
    lo_lim = jnp.where(i > 0, -POOL_HALO, 0)
    hi_lim = jnp.where(i < nblk - 1, tm + POOL_HALO - 1, tm - 1)
    row_ok = (erow >= lo_lim) & (erow <= hi_lim)
    pg = pw_ref.shape[1]
    outs = []
    for gi, window in enumerate(POOL_WINDOWS):
        lo = window // 2
        hi = window - 1 - lo
        seg = u[:, ds5 + gi * pg:ds5 + (gi + 1) * pg]
        ext = jnp.concatenate(
            [prev_ref[:, gi * pg:(gi + 1) * pg], seg, next_ref[:, gi * pg:(gi + 1) * pg],
             jnp.zeros((ext_rows - tm - 2 * POOL_HALO, pg), F32)], axis=0)
        ext = jnp.where(row_ok, ext, 0.0)
        band = jnp.where((diff >= -lo) & (diff <= hi), 1.0, 0.0).astype(BF16)
        wsum = sum(jnp.dot(band, part, preferred_element_type=F32) for part in _split3(ext))
        count = (jnp.minimum(tcol + hi, hi_lim) - jnp.maximum(tcol - lo, lo_lim) + 1).astype(F32)
        resid = wsum / count - seg
        outs.append(jnp.dot(resid.astype(BF16), pw_ref[gi], preferred_element_type=F32))
    ybp = jnp.concatenate(outs, axis=1) * ps_ref[...]
    cat = jnp.concatenate([ya, ybp], axis=1).astype(BF16)
    mix = jnp.dot(cat, wo_ref[...], preferred_element_type=F32)
    o_ref[...] = x_ref[...] + mod_ref[2:3, :] * _rms(mix, g_ref[...])


def _mixout(x, u, yf, yb, mod, g, d, glu_w, pool_w, pool_scale, w_out, tm):
    n, dm = x.shape
    ds5 = yf.shape[1]
    nblk = n // tm
    hb = tm // POOL_HALO
    nhalo = n // POOL_HALO
    prev_map = lambda i: (jnp.maximum(i * hb - 1, 0), 1)
    next_map = lambda i: (jnp.minimum((i + 1) * hb, nhalo - 1), 1)
    row = lambda i: (i, 0)
    return pl.pallas_call(
        functools.partial(_mixout_kernel, nblk=nblk),
        grid=(nblk,),
        in_specs=[pl.BlockSpec((tm, dm), row), pl.BlockSpec((tm, dm), row),
                  pl.BlockSpec((tm, ds5), row), pl.BlockSpec((tm, ds5), row),
                  pl.BlockSpec((POOL_HALO, dm - ds5), prev_map), pl.BlockSpec((POOL_HALO, dm - ds5), next_map),
                  _full(mod.shape), _full(g.shape), _full(d.shape), _full(glu_w.shape),
                  _full(pool_w.shape), _full(pool_scale.shape), _full(w_out.shape)],
        out_specs=pl.BlockSpec((tm, dm), row),
        out_shape=jax.ShapeDtypeStruct((n, dm), F32),
        compiler_params=_cparams(("arbitrary",)),
        name="mixout",
    )(x, u, yf, yb, u, u, mod, g, d, glu_w, pool_w, pool_scale, w_out)


def _mlp_tail(x1, mod_ref, g2, g3, w1_ref, w2_ref):
    dff = w1_ref.shape[1]
    fc = 1024
    hb = (_rms(x1, g2) * (1.0 + mod_ref[4:5, :]) + mod_ref[3:4, :]).astype(BF16)
    acc = jnp.zeros(x1.shape, F32)
    for c in range(dff // fc):
        h1 = jnp.dot(hb, w1_ref[:, c * fc:(c + 1) * fc], preferred_element_type=F32)
        h1 = jnp.square(jnp.maximum(h1, 0.0)).astype(BF16)
        acc = acc + jnp.dot(h1, w2_ref[c * fc:(c + 1) * fc, :], preferred_element_type=F32)
    return x1 + mod_ref[5:6, :] * _rms(acc, g3)


def _mlp_kernel(x_ref, mod_ref, g_ref, w1_ref, w2_ref, o_ref):
    o_ref[...] = _mlp_tail(x_ref[...], mod_ref, g_ref[2:3, :], g_ref[3:4, :], w1_ref, w2_ref)


def _attnout_mlp_kernel(x_ref, a_ref, wo_ref, mod_ref, g_ref, w1_ref, w2_ref, o_ref):
    y = jnp.dot(a_ref[...], wo_ref[...], preferred_element_type=F32)
    x1 = x_ref[...] + mod_ref[2:3, :] * _rms(y, g_ref[1:2, :])
    o_ref[...] = _mlp_tail(x1, mod_ref, g_ref[2:3, :], g_ref[3:4, :], w1_ref, w2_ref)


def _mlp(x, mod, g4, w1, w2, tm, attn=None, w_out=None):
    n, d = x.shape
    row = lambda i: (i, 0)
    if attn is None:
        kern = _mlp_kernel
        args = (x, mod, g4, w1, w2)
        specs = [pl.BlockSpec((tm, d), row), _full(mod.shape), _full(g4.shape), _full(w1.shape), _full(w2.shape)]
    else:
        kern = _attnout_mlp_kernel
        args = (x, attn, w_out, mod, g4, w1, w2)
        specs = [pl.BlockSpec((tm, d), row), pl.BlockSpec((tm, attn.shape[1]), row), _full(w_out.shape),
                 _full(mod.shape), _full(g4.shape), _full(w1.shape), _full(w2.shape)]
    return pl.pallas_call(
        kern,
        grid=(n // tm,),
        in_specs=specs,
        out_specs=pl.BlockSpec((tm, d), row),
        out_shape=jax.ShapeDtypeStruct((n, d), F32),
        compiler_params=_cparams(("arbitrary",)),
        name="mlp" if attn is None else "attnout_mlp",
    )(*args)


def _swap16(x):
    lane = lax.broadcasted_iota(jnp.int32, x.shape, 1)
    return jnp.where((lane % 32) < 16, pltpu.roll(x, LANES - 16, 1), pltpu.roll(x, 16, 1))


def _head_rms(x, ones_bd, gain):
    sq = x * x
    hi = sq.astype(BF16)
    lo = (sq - hi.astype(F32)).astype(BF16)
    ssq = jnp.dot(hi, ones_bd, preferred_element_type=F32) + jnp.dot(lo, ones_bd, preferred_element_type=F32)
    return x * lax.rsqrt(ssq * (1.0 / HEAD_DIM) + EPS) * gain


def _rope(x, cos, sin):
    tiles = []
    for j in range(x.shape[1] // LANES):
        xt = x[:, j * LANES:(j + 1) * LANES]
        tiles.append(xt * cos + _swap16(xt) * sin)
    return jnp.concatenate(tiles, axis=1)


def _qkv_kernel(x_ref, mod_ref, g_ref, w_ref, qn_ref, kn_ref, cos_ref, sin_ref, oq_ref, ok_ref, eq_ref, ekv_ref,
                qt_ref, k_ref, vt_ref, *, rope):
    dq = oq_ref.shape[0]
    dkv = ok_ref.shape[0]
    h = _rms(x_ref[...], g_ref[...]) * (1.0 + mod_ref[1:2, :]) + mod_ref[0:1, :]
    qkv = jnp.dot(h.astype(BF16), w_ref[...], preferred_element_type=F32)
    q = _head_rms(qkv[:, :dq], oq_ref[...], qn_ref[...])
    k = _head_rms(qkv[:, dq:dq + dkv], ok_ref[...], kn_ref[...])
    v = qkv[:, dq + dkv:]
    if rope:
        q = _rope(q, cos_ref[...], sin_ref[...])
        k = _rope(k, cos_ref[...], sin_ref[...])
    qb = (q * (HEAD_DIM ** -0.5 * LOG2E)).astype(BF16)
    nt = (((1,), (1,)), ((), ()))
    qt_ref[...] = lax.dot_general(eq_ref[...], qb, nt, preferred_element_type=F32).astype(BF16)
    vt = lax.dot_general(ekv_ref[...], v.astype(BF16), nt, preferred_element_type=F32).astype(BF16)
    kb = k.astype(BF16)
    tm = x_ref.shape[0]
    pad = vt_ref.shape[1] - HEAD_DIM
    extra = jnp.where(lax.broadcasted_iota(jnp.int32, (pad, tm), 0) == 0, 1.0, 0.0).astype(BF16)
    for g in range(k_ref.shape[0]):
        k_ref[g] = kb[:, g * HEAD_DIM:(g + 1) * HEAD_DIM]
        vt_ref[g, :HEAD_DIM, :] = vt[g * HEAD_DIM:(g + 1) * HEAD_DIM, :]
        vt_ref[g, HEAD_DIM:, :] = extra


def _qkv(x, mod, g, w, qn, kn, cos, sin, consts, tm, rope):
    n, d = x.shape
    ones_q, ones_k, eye_q, eye_kv = consts
    dq = ones_q.shape[0]
    dkv = ones_k.shape[0]
    nkv = dkv // HEAD_DIM
    row = lambda i: (i, 0)
    return pl.pallas_call(
        functools.partial(_qkv_kernel, rope=rope),
        grid=(n // tm,),
        in_specs=[pl.BlockSpec((tm, d), row), _full(mod.shape), _full(g.shape), _full(w.shape),
                  _full(qn.shape), _full(kn.shape),
                  pl.BlockSpec((tm, LANES), row), pl.BlockSpec((tm, LANES), row),
                  _full(ones_q.shape), _full(ones_k.shape), _full(eye_q.shape), _full(eye_kv.shape)],
        out_specs=[pl.BlockSpec((dq, tm), lambda i: (0, i)),
                   pl.BlockSpec((nkv, tm, HEAD_DIM), lambda i: (0, i, 0)),
                   pl.BlockSpec((nkv, V_ROWS, tm), lambda i: (0, 0, i))],
        out_shape=[jax.ShapeDtypeStruct((dq, n), BF16),
                   jax.ShapeDtypeStruct((nkv, n, HEAD_DIM), BF16),
                   jax.ShapeDtypeStruct((nkv, V_ROWS, n), BF16)],
        compiler_params=_cparams(("arbitrary",)),
        name="qkv_rope" if rope else "qkv_ctx",
    )(x, mod, g, w, qn, kn, cos, sin, ones_q, ones_k, eye_q, eye_kv)


def _qkv_constants(dq, dkv):
    def ones_bd(n):
        idx = jnp.arange(n) // HEAD_DIM
        return (idx[:, None] == idx[None, :]).astype(BF16)
    return ones_bd(dq), ones_bd(dkv), jnp.eye(dq, dtype=BF16), jnp.eye(dkv, dtype=BF16)


def _rope_tables(n_lat):
    rows = n_lat // GRID_W
    row_ids = jnp.repeat(jnp.arange(rows), GRID_W).astype(F32)
    col_ids = jnp.tile(jnp.arange(GRID_W), rows).astype(F32)
    axis_dim = HEAD_DIM // 2
    inv_freq = ROPE_THETA ** (-jnp.arange(0, axis_dim, 2, dtype=F32) / axis_dim)
    ang_r = row_ids[:, None] * inv_freq[None, :]
    ang_c = col_ids[:, None] * inv_freq[None, :]
    cos = jnp.concatenate([jnp.cos(ang_r)] * 2 + [jnp.cos(ang_c)] * 2, axis=1)
    sin = jnp.concatenate([-jnp.sin(ang_r), jnp.sin(ang_r), -jnp.sin(ang_c), jnp.sin(ang_c)], axis=1)
    reps = LANES // HEAD_DIM
    return jnp.tile(cos, (1, reps)), jnp.tile(sin, (1, reps))


def _attn_kernel(qt_ref, kc_ref, vtc_ref, kl_ref, vtl_ref, o_ref, pa_scr, pb_scr, al_scr, m_scr, acc_scr,
                 *, tk, shift):
    tq = qt_ref.shape[1]
    n_lat = kl_ref.shape[1]
    nblk = n_lat // tk
    acc_scr[...] = jnp.zeros(acc_scr.shape, F32)
    if shift:
        m_scr[...] = jnp.full(m_scr.shape, -jnp.inf, F32)

    def probs(k):
        out = []
        for h in range(KV_GROUP):
            s = jnp.dot(k, qt_ref[h * HEAD_DIM:(h + 1) * HEAD_DIM, :], preferred_element_type=F32)
            if shift:
                m_old = m_scr[h]
                m_new = jnp.maximum(m_old, jnp.max(s, axis=0, keepdims=True))
                m_scr[h] = m_new
                out.append((jnp.exp2(s - m_new).astype(BF16), jnp.exp2(m_old - m_new)))
            else:
                out.append((jnp.exp2(s).astype(BF16), None))
        return out

    def weighted(ps, vt):
        for h, (p, alpha) in enumerate(ps):
            pv = jnp.dot(vt, p, preferred_element_type=F32)
            acc_scr[h] = (acc_scr[h] * alpha + pv) if shift else (acc_scr[h] + pv)

    def kblk(j):
        return kl_ref[0, pl.ds(pl.multiple_of(j * tk, tk), tk), :]

    def vblk(j):
        return vtl_ref[0, :, pl.ds(pl.multiple_of(j * tk, tk), tk)]

    def fill(slot, k):
        scr = (pa_scr, pb_scr)[slot]
        for h, (p, alpha) in enumerate(probs(k)):
            scr[h] = p
            if shift:
                al_scr[slot, h] = alpha

    def drain(slot, vt):
        scr = (pa_scr, pb_scr)[slot]
        weighted([(scr[h], al_scr[slot, h] if shift else None) for h in range(KV_GROUP)], vt)

    weighted(probs(kc_ref[0]), vtc_ref[0])
    fill(0, kblk(0))

    def body(i, _):
        fill(1, kblk(2 * i + 1))
        drain(0, vblk(2 * i))
        fill(0, kblk(2 * i + 2))
        drain(1, vblk(2 * i + 1))
        return 0

    lax.fori_loop(0, nblk // 2 - 1, body, 0)
    fill(1, kblk(nblk - 1))
    drain(0, vblk(nblk - 2))
    drain(1, vblk(nblk - 1))
    outs = [acc_scr[h, :HEAD_DIM, :] / acc_scr[h, HEAD_DIM:HEAD_DIM + 1, :] for h in range(KV_GROUP)]
    o_ref[...] = jnp.transpose(jnp.concatenate(outs, axis=0)).astype(o_ref.dtype)


def _attention(qt, k_ctx, vt_ctx, k_lat, vt_lat, score_bound, tq, tk):
    dq, n = qt.shape
    nkv, n_ctx, _ = k_ctx.shape
    wide = KV_GROUP * HEAD_DIM
    assert (n // tk) % 2 == 0

    def call(shift, name):
        once = dict(pipeline_mode=pl.Buffered(1))
        return pl.pallas_call(
            functools.partial(_attn_kernel, tk=tk, shift=shift),
            grid=(nkv, n // tq),
            in_specs=[pl.BlockSpec((wide, tq), lambda g, i: (g, i)),
                      pl.BlockSpec((1, n_ctx, HEAD_DIM), lambda g, i: (g, 0, 0)),
                      pl.BlockSpec((1, V_ROWS, n_ctx), lambda g, i: (g, 0, 0)),
                      pl.BlockSpec((1, n, HEAD_DIM), lambda g, i: (g, 0, 0), **once),
                      pl.BlockSpec((1, V_ROWS, n), lambda g, i: (g, 0, 0), **once)],
            out_specs=pl.BlockSpec((tq, wide), lambda g, i: (i, g)),
            out_shape=jax.ShapeDtypeStruct((n, dq), BF16),
            scratch_shapes=[pltpu.VMEM((KV_GROUP, tk, tq), BF16),
                            pltpu.VMEM((KV_GROUP, tk, tq), BF16),
                            pltpu.VMEM((2, KV_GROUP, 1, tq), F32),
                            pltpu.VMEM((KV_GROUP, 1, tq), F32),
                            pltpu.VMEM((KV_GROUP, V_ROWS, tq), F32)],
            compiler_params=_cparams(("arbitrary", "arbitrary")),
            name=name,
        )(qt, k_ctx, vt_ctx, k_lat, vt_lat)

    return lax.cond(score_bound <= SAFE_EXP2_BOUND,
                    lambda: call(False, "gqa_flash_noshift"), lambda: call(True, "gqa_flash"))


def kernel(x, c, ctx, c_ctx, ada_w, ada_b, norm_g, mlp_w1, mlp_w2, mix_in_w, mix_out_w, s5_lambda_re, s5_lambda_im, s5_log_dt, s5_b_re, s5_b_im, s5_c_re, s5_c_im, s5_d, s5_glu_w, pool_w, pool_scale, attn_qkv_w, attn_out_w, attn_q_norm, attn_k_norm):
    assert x.shape[0] == 1 and ctx.shape[0] == 1
    xl = x[0]
    xc = ctx[0]
    n_lat, d = xl.shape
    n_ctx = xc.shape[0]
    assert n_ctx == CHUNK and n_lat % 1024 == 0 and n_lat % GRID_W == 0
    tm = 512

    mod = _modulation(c, c_ctx, ada_w, ada_b)
    w1 = mlp_w1.astype(BF16)
    w2 = mlp_w2.astype(BF16)

    g0 = norm_g[0]
    w_in = mix_in_w[0].astype(BF16)
    u_lat = _inproj(xl, mod[0, 0], g0[0:1], w_in, tm)
    u_ctx = _inproj(xc, mod[0, 1], g0[0:1], w_in, n_ctx)
    a8, apow, bmat, cmat = _s5_prepare(s5_lambda_re[0], s5_lambda_im[0], s5_log_dt[0], s5_b_re[0], s5_b_im[0],
                                       s5_c_re[0], s5_c_im[0])
    yf_ctx, yf_lat, yb_ctx, yb_lat = _s5_scan(u_ctx, u_lat, a8, apow, bmat, cmat)
    mix_args = (s5_d[0].reshape(1, -1), s5_glu_w[0].astype(BF16), pool_w[0].astype(BF16),
                pool_scale[0].reshape(1, -1), mix_out_w[0].astype(BF16))
    xl = _mixout(xl, u_lat, yf_lat, yb_lat, mod[0, 0], g0[1:2], *mix_args, tm=CHUNK)
    xc = _mixout(xc, u_ctx, yf_ctx, yb_ctx, mod[0, 1], g0[1:2], *mix_args, tm=CHUNK)
    xl = _mlp(xl, mod[0, 0], g0, w1[0], w2[0], tm)
    xc = _mlp(xc, mod[0, 1], g0, w1[0], w2[0], n_ctx)

    g1 = norm_g[1]
    w_qkv = attn_qkv_w[0].astype(BF16)
    dkv = (w_qkv.shape[1] - d) // 2
    consts = _qkv_constants(d, dkv)
    reps = d // HEAD_DIM
    qn = jnp.tile(attn_q_norm[0], reps).reshape(1, -1)
    kn = jnp.tile(attn_k_norm[0], dkv // HEAD_DIM).reshape(1, -1)
    cos, sin = _rope_tables(n_lat)
    qt, k_lat, vt_lat = _qkv(xl, mod[1, 0], g1[0:1], w_qkv, qn, kn, cos, sin, consts, tm, rope=True)
    _, k_ctx, vt_ctx = _qkv(xc, mod[1, 1], g1[0:1], w_qkv, qn, kn, cos[:n_ctx], sin[:n_ctx], consts, n_ctx, rope=False)
    score_bound = (1.05 * LOG2E * HEAD_DIM ** 0.5
                   * jnp.max(jnp.abs(attn_q_norm[0])) * jnp.max(jnp.abs(attn_k_norm[0])))
    att = _attention(qt, k_ctx, vt_ctx, k_lat, vt_lat, score_bound, tq=512, tk=1024)
    xl = _mlp(xl, mod[1, 0], g1, w1[1], w2[1], tm, attn=att, w_out=attn_out_w[0].astype(BF16))
    return xl[None]
```
